```python
import math
import jax
import jax.numpy as jnp
from jax import lax
import numpy as np

D_MODEL = 1024
BATCH = 4
SEQ = 4096
DEPTH = 2

RWKV_HEAD_DIM = 64
RWKV_WIDTH = D_MODEL // 2
RWKV_HEADS = RWKV_WIDTH // RWKV_HEAD_DIM
GDN_HEAD_DIM = 128
GDN_WIDTH = D_MODEL - RWKV_WIDTH
GDN_HEADS = GDN_WIDTH // GDN_HEAD_DIM
MIX_WIDTH = RWKV_WIDTH + GDN_WIDTH
DECAY_RANK = 64
ICLR_RANK = 64
VRES_RANK = 32
GATE_RANK = 128
RWKV_COLS = 3 * RWKV_WIDTH + DECAY_RANK + ICLR_RANK + GATE_RANK
GDN_COLS = 4 * GDN_WIDTH + 2 * GDN_HEADS
IN_COLS = RWKV_COLS + GDN_COLS
CONV_WIDTH = 4
CHUNK = 64
D_FF = 2816
N_SUB = 3
MACARON_WEIGHT = 0.5
EPS_RMS = 1e-6
EPS_GN = 64e-5
EPS_L2 = 1e-6

kernel_name = 'hybrid_rwkv7_gdn_macaron_adaln'


def _split_cols(y, sizes):
    out, start = [], 0
    for s in sizes:
        out.append(y[..., start:start + s])
        start += s
    return out


def _rms_norm(x, gain):
    xf = x.astype(jnp.float32)
    y = xf * lax.rsqrt(jnp.mean(xf * xf, axis=-1, keepdims=True) + EPS_RMS)
    return (y * gain.astype(jnp.float32)).astype(x.dtype)


def _modulate(h, shift, scale):
    return h * (1.0 + scale[:, None, :]) + shift[:, None, :]


def _swiglu(h, w_gu, w_dn):
    gt, up = jnp.split(h @ w_gu, 2, axis=-1)
    return (jax.nn.silu(gt) * up) @ w_dn


def _l2norm(t, eps):
    return t * lax.rsqrt(jnp.sum(t * t, axis=-1, keepdims=True) + eps)


def _token_shift_mix(y, mu):
    prev = jnp.pad(y, ((0, 0), (1, 0), (0, 0)))[:, :-1]
    return y + (prev - y) * mu


def _causal_dwconv(x, w):
    return lax.conv_general_dilated(
        x, w.astype(x.dtype)[:, None, :], window_strides=(1,),
        padding=((w.shape[0] - 1, 0),), dimension_numbers=('NWC', 'WIO', 'NWC'),
        feature_group_count=x.shape[-1])


def _rwkv7_scan(r, decay, k, v, kk, b):
    B, S, H, N = r.shape

    def step(state, inp):
        r_t, w_t, k_t, v_t, kk_t, b_t = inp
        sa = jnp.einsum('bhvk,bhk->bhv', state, -kk_t)
        state = (state * w_t[:, :, None, :] + sa[..., None] * b_t[:, :, None, :]
                 + v_t[..., None] * k_t[:, :, None, :])
        return state, jnp.einsum('bhvk,bhk->bhv', state, r_t)

    xs = tuple(jnp.moveaxis(t, 1, 0) for t in (r, decay, k, v, kk, b))
    _, o = lax.scan(step, jnp.zeros((B, H, N, N), jnp.float32), xs)
    return jnp.moveaxis(o, 0, 1)


def rwkv7_time_mix(y, v_first, vres_logit, w0, w_up, a0, a_up, g_up, k_k, k_a, r_k, gn_w, gn_b):
    B, S, _ = y.shape
    H, N = RWKV_HEADS, RWKV_HEAD_DIM
    r, k, v, wd, ad, gd = _split_cols(
        y.astype(jnp.float32),
        (RWKV_WIDTH, RWKV_WIDTH, RWKV_WIDTH, DECAY_RANK, ICLR_RANK, GATE_RANK))
    if v_first is not None:
        v = v + (v_first - v) * jax.nn.sigmoid(vres_logit)
    w_log = -jax.nn.softplus(-(w0 + jnp.tanh(wd) @ w_up)) - 0.5
    decay = jnp.exp(-jnp.exp(w_log))
    a = jax.nn.sigmoid(a0 + ad @ a_up)
    g = jax.nn.sigmoid(gd) @ g_up
    heads = lambda t: t.reshape(B, S, H, N)
    kk = _l2norm(heads(k * k_k), EPS_L2)
    k = k * (1.0 + (a - 1.0) * k_a)
    rh, kh, vh = heads(r), heads(k), heads(v)
    o = _rwkv7_scan(rh, heads(decay), kh, vh, kk, kk * heads(a))
    mu = jnp.mean(o, axis=-1, keepdims=True)
    var = jnp.mean(jnp.square(o - mu), axis=-1, keepdims=True)
    o = (o - mu) * lax.rsqrt(var + EPS_GN) * gn_w.reshape(H, N) + gn_b.reshape(H, N)
    o = o + jnp.sum(rh * kh * r_k.reshape(H, N), axis=-1, keepdims=True) * vh
    return o.reshape(B, S, RWKV_WIDTH) * g, v


def _chunk_gated_delta(q, k, v, g, beta):
    B, S, H, DK = q.shape
    DV = v.shape[-1]
    n = S // CHUNK

    def blocks(t):
        return jnp.moveaxis(t.reshape((B, n, CHUNK, H) + t.shape[3:]), 3, 1)

    q, k, v, g, beta = blocks(q), blocks(k), blocks(v), blocks(g), blocks(beta)
    cum = jnp.cumsum(g, axis=-1)
    incl = jnp.tril(jnp.ones((CHUNK, CHUNK), bool))
    strict = jnp.tril(jnp.ones((CHUNK, CHUNK), bool), -1)
    diff = cum[..., :, None] - cum[..., None, :]
    decay = jnp.where(incl, jnp.exp(jnp.where(incl, diff, 0.0)), 0.0)
    kk = jnp.einsum('bhnid,bhnjd->bhnij', k, k)
    a_mat = jnp.where(strict, kk * decay * beta[..., :, None], 0.0) + jnp.eye(CHUNK, dtype=k.dtype)
    rhs = jnp.concatenate([v * beta[..., None], k * (beta * jnp.exp(cum))[..., None]], axis=-1)
    sol = lax.linalg.triangular_solve(a_mat, rhs, left_side=True, lower=True, unit_diagonal=True)
    u, w = sol[..., :DV], sol[..., DV:]
    p = jnp.einsum('bhnid,bhnjd->bhnij', q, k) * decay
    q_dec = q * jnp.exp(cum)[..., None]
    k_dec = k * jnp.exp(cum[..., -1:] - cum)[..., None]
    g_tot = jnp.exp(cum[..., -1])

    def step(state, inp):
        u_c, w_c, p_c, q_c, k_c, g_c = inp
        v_new = u_c - jnp.einsum('bhck,bhkv->bhcv', w_c, state)
        o_c = jnp.einsum('bhck,bhkv->bhcv', q_c, state) + jnp.einsum('bhcj,bhjv->bhcv', p_c, v_new)
        state = state * g_c[..., None, None] + jnp.einsum('bhck,bhcv->bhkv', k_c, v_new)
        return state, o_c

    xs = tuple(jnp.moveaxis(t, 2, 0) for t in (u, w, p, q_dec, k_dec, g_tot))
    _, o = lax.scan(step, jnp.zeros((B, H, DK, DV), jnp.float32), xs)
    o = jnp.moveaxis(o, 0, 2).reshape(B, H, S, DV)
    return jnp.moveaxis(o, 1, 2)


def gated_deltanet(y, conv_w, a_log, dt_bias, norm_w):
    B, S, _ = y.shape
    H, Dh = GDN_HEADS, GDN_HEAD_DIM
    qkv, z, b_logit, a_logit = _split_cols(y, (3 * GDN_WIDTH, GDN_WIDTH, H, H))
    qkv = jax.nn.silu(_causal_dwconv(qkv, conv_w)).astype(jnp.float32)
    q, k, v = [t.reshape(B, S, H, Dh) for t in jnp.split(qkv, 3, axis=-1)]
    q = _l2norm(q, EPS_L2) * (Dh ** -0.5)
    k = _l2norm(k, EPS_L2)
    beta = jax.nn.sigmoid(b_logit.astype(jnp.float32))
    g = -jnp.exp(a_log.astype(jnp.float32)) * jax.nn.softplus(a_logit.astype(jnp.float32) + dt_bias)
    o = _chunk_gated_delta(q, k, v, g, beta)
    o = o * lax.rsqrt(jnp.mean(o * o, axis=-1, keepdims=True) + EPS_RMS) * norm_w
    return o.reshape(B, S, GDN_WIDTH) * jax.nn.silu(z.astype(jnp.float32))


def setup_inputs(seed: int = 0) -> dict:
    key = jax.random.key(seed)
    ks = jax.random.split(key, 32)
    nrm = lambda k, shape, scale: jax.random.normal(k, shape, jnp.float32) * scale
    L, D, RW, V = DEPTH, D_MODEL, RWKV_WIDTH, DEPTH - 1
    dt = jnp.exp(jax.random.uniform(ks[26], (L, GDN_HEADS), minval=math.log(1e-3), maxval=math.log(1e-1)))
    return {
        'x': nrm(ks[0], (BATCH, SEQ, D), 1.0),
        'c': nrm(ks[1], (BATCH, D), 1.0),
        'norm_gain': 1.0 + nrm(ks[2], (L, N_SUB, D), 0.02),
        'ada_w': nrm(ks[3], (L, N_SUB, D, 3 * D), 0.5 * D ** -0.5),
        'ada_b': nrm(ks[4], (L, N_SUB, 3 * D), 0.02),
        'ffn_w_gu': nrm(ks[5], (L, 2, D, 2 * D_FF), D ** -0.5),
        'ffn_w_down': nrm(ks[6], (L, 2, D_FF, D), D_FF ** -0.5),
        'w_in': nrm(ks[7], (L, D, IN_COLS), D ** -0.5),
        'w_out': nrm(ks[8], (L, MIX_WIDTH, D), MIX_WIDTH ** -0.5),
        'rwkv_mu': jax.random.uniform(ks[9], (L, RWKV_COLS)),
        'rwkv_w0': jax.random.uniform(ks[10], (L, RW), minval=-5.0, maxval=1.0),
        'rwkv_w_up': nrm(ks[11], (L, DECAY_RANK, RW), 0.1 * DECAY_RANK ** -0.5),
        'rwkv_a0': nrm(ks[12], (L, RW), 0.1),
        'rwkv_a_up': nrm(ks[13], (L, ICLR_RANK, RW), 0.1 * ICLR_RANK ** -0.5),
        'rwkv_g_up': nrm(ks[14], (L, GATE_RANK, RW), GATE_RANK ** -0.5),
        'rwkv_k_k': 0.85 + nrm(ks[15], (L, RW), 0.02),
        'rwkv_k_a': 1.0 + nrm(ks[16], (L, RW), 0.02),
        'rwkv_r_k': -0.04 + nrm(ks[17], (L, RW), 0.02),
        'rwkv_gn_w': 1.0 + nrm(ks[18], (L, RW), 0.02),
        'rwkv_gn_b': nrm(ks[19], (L, RW), 0.02),
        'vres_w_down': nrm(ks[20], (V, D, VRES_RANK), D ** -0.5),
        'vres_mu': jax.random.uniform(ks[21], (V, VRES_RANK)),
        'vres_w_up': nrm(ks[22], (V, VRES_RANK, RW), 0.1 * VRES_RANK ** -0.5),
        'vres_v0': 1.0 + nrm(ks[23], (V, RW), 0.1),
        'gdn_conv_w': nrm(ks[24], (L, CONV_WIDTH, 3 * GDN_WIDTH), CONV_WIDTH ** -0.5),
        'gdn_a_log': jnp.log(jax.random.uniform(ks[25], (L, GDN_HEADS), minval=1.0, maxval=16.0)),
        'gdn_dt_bias': dt + jnp.log(-jnp.expm1(-dt)),
        'gdn_norm_w': 1.0 + nrm(ks[27], (L, GDN_HEAD_DIM), 0.02),
        'final_gain': 1.0 + nrm(ks[28], (D,), 0.02),
    }


def reference(x, c, norm_gain, ada_w, ada_b, ffn_w_gu, ffn_w_down, w_in, w_out,
              rwkv_mu, rwkv_w0, rwkv_w_up, rwkv_a0, rwkv_a_up, rwkv_g_up, rwkv_k_k, rwkv_k_a,
              rwkv_r_k, rwkv_gn_w, rwkv_gn_b, vres_w_down, vres_mu, vres_w_up, vres_v0,
              gdn_conv_w, gdn_a_log, gdn_dt_bias, gdn_norm_w, final_gain):
    c_act = jax.nn.silu(c)
    v_first = None
    for l in range(DEPTH):
        mod = jnp.einsum('bd,sde->sbe', c_act, ada_w[l]) + ada_b[l][:, None, :]
        shift, scale, gate = jnp.split(mod, 3, axis=-1)

        h = _modulate(_rms_norm(x, norm_gain[l, 0]), shift[0], scale[0])
        x = x + MACARON_WEIGHT * gate[0][:, None, :] * _swiglu(h, ffn_w_gu[l, 0], ffn_w_down[l, 0])

        h = _modulate(_rms_norm(x, norm_gain[l, 1]), shift[1], scale[1])
        if l == 0:
            y = h @ w_in[l]
        else:
            y = h @ jnp.concatenate([w_in[l], vres_w_down[l - 1]], axis=-1)
        y_rwkv = _token_shift_mix(y[..., :RWKV_COLS], rwkv_mu[l])
        y_gdn = y[..., RWKV_COLS:RWKV_COLS + GDN_COLS]
        if l == 0:
            vres_logit = None
        else:
            y_vres = _token_shift_mix(y[..., RWKV_COLS + GDN_COLS:], vres_mu[l - 1]).astype(jnp.float32)
            vres_logit = vres_v0[l - 1] + y_vres @ vres_w_up[l - 1]
        o_rwkv, v_l = rwkv7_time_mix(y_rwkv, v_first, vres_logit, rwkv_w0[l], rwkv_w_up[l],
                                     rwkv_a0[l], rwkv_a_up[l], rwkv_g_up[l], rwkv_k_k[l],
                                     rwkv_k_a[l], rwkv_r_k[l], rwkv_gn_w[l], rwkv_gn_b[l])
        if l == 0:
            v_first = v_l
        o_gdn = gated_deltanet(y_gdn, gdn_conv_w[l], gdn_a_log[l], gdn_dt_bias[l], gdn_norm_w[l])
        mixed = jnp.concatenate([o_rwkv, o_gdn], axis=-1).astype(x.dtype) @ w_out[l]
        x = x + gate[1][:, None, :] * mixed

        h = _modulate(_rms_norm(x, norm_gain[l, 2]), shift[2], scale[2])
        x = x + MACARON_WEIGHT * gate[2][:, None, :] * _swiglu(h, ffn_w_gu[l, 1], ffn_w_down[l, 1])
    return _rms_norm(x, final_gain)
```

```python
import functools

import jax
import jax.numpy as jnp
from jax import lax
from jax.experimental import pallas as pl
from jax.experimental.pallas import tpu as pltpu

F32 = jnp.float32
BF16 = jnp.bfloat16
HI = lax.Precision.HIGHEST

RWKV_HEAD_DIM = 64
RWKV_HEADS = 8
RWKV_WIDTH = RWKV_HEAD_DIM * RWKV_HEADS
GDN_HEAD_DIM = 128
GDN_HEADS = 4
GDN_WIDTH = GDN_HEAD_DIM * GDN_HEADS
DECAY_RANK = 64
ICLR_RANK = 64
VRES_RANK = 32
GATE_RANK = 128
RWKV_COLS = 3 * RWKV_WIDTH + DECAY_RANK + ICLR_RANK + GATE_RANK
GDN_QKV = 3 * GDN_WIDTH
CONV_WIDTH = 4
CHUNK = 64
LANE = 128
SUBLANE = 8
EPS_RMS = 1e-6
EPS_GN = 64e-5
EPS_L2 = 1e-6
MACARON_WEIGHT = 0.5
VMEM_LIMIT = 52 * 1024 * 1024

NT = (((1,), (1,)), ((), ()))
TN = (((0,), (0,)), ((), ()))


def _sigmoid(x):
    return 1.0 / (1.0 + jnp.exp(-x))


def _softplus(x):
    return jnp.maximum(x, 0.0) + jnp.log(1.0 + jnp.exp(-jnp.abs(x)))


def _silu(x):
    return x * _sigmoid(x)


def _dot(a, b, precision=None):
    return jnp.dot(a, b, preferred_element_type=F32, precision=precision)


def _dg(a, b, dims, precision=None):
    return lax.dot_general(a, b, dims, preferred_element_type=F32, precision=precision)


def _norm_mod(x, gain, shift, scale):
    ms = jnp.mean(x * x, axis=-1, keepdims=True)
    y = x * lax.rsqrt(ms + EPS_RMS) * gain
    return y * (1.0 + scale) + shift


def _tri(n, strict):
    r = lax.broadcasted_iota(jnp.int32, (n, n), 0)
    c = lax.broadcasted_iota(jnp.int32, (n, n), 1)
    return (r > c) if strict else (r >= c)


def _unit_lower_solve(neg_strict, rhs):
    n = neg_strict
    x = rhs
    steps = CHUNK.bit_length() - 1
    for i in range(steps):
        x = x + _dot(n, x, HI)
        if i + 1 < steps:
            n = _dot(n, n, HI)
    return x


def _mod_kernel(c_ref, w_ref, b_ref, o_ref):
    ca = _silu(c_ref[...])
    o_ref[0] = _dot(ca, w_ref[0], HI) + b_ref[0]


def _adaln(c, ada_w, ada_b):
    n_l, n_sub, d, d3 = ada_w.shape
    b = c.shape[0]
    rows = -(-b // SUBLANE) * SUBLANE
    c_pad = jnp.pad(c, ((0, rows - b), (0, 0)))
    tn = d
    out = pl.pallas_call(
        _mod_kernel,
        grid=(n_l * n_sub, d3 // tn),
        in_specs=[
            pl.BlockSpec((rows, d), lambda i, j: (0, 0)),
            pl.BlockSpec((1, d, tn), lambda i, j: (i, 0, j)),
            pl.BlockSpec((1, 1, tn), lambda i, j: (i, 0, j)),
        ],
        out_specs=pl.BlockSpec((1, rows, tn), lambda i, j: (i, 0, j)),
        out_shape=jax.ShapeDtypeStruct((n_l * n_sub, rows, d3), F32),
        compiler_params=pltpu.CompilerParams(
            dimension_semantics=("parallel", "parallel"), vmem_limit_bytes=VMEM_LIMIT),
        name="adaln_mod",
    )(c_pad, ada_w.reshape(n_l * n_sub, d, d3), ada_b.reshape(n_l * n_sub, 1, d3))
    return out[:, :b, :]


def _ffn_kernel(x_ref, gain_ref, sh_ref, sc_ref, gt_ref, wg_ref, wu_ref, wd_ref, *rest, final):
    if final:
        fg_ref, o_ref, h_scr, acc = rest
    else:
        o_ref, h_scr, acc = rest
    j = pl.program_id(1)

    @pl.when(j == 0)
    def _():
        h = _norm_mod(x_ref[...], gain_ref[...], sh_ref[0], sc_ref[0])
        h_scr[...] = h.astype(BF16)
        acc[...] = jnp.zeros_like(acc)

    h = h_scr[...]
    g = _dot(h, wg_ref[...])
    u = _dot(h, wu_ref[...])
    act = (_silu(g) * u).astype(BF16)
    acc[...] += _dot(act, wd_ref[...])

    @pl.when(j == pl.num_programs(1) - 1)
    def _():
        y = x_ref[...] + (MACARON_WEIGHT * gt_ref[0]) * acc[...]
        if final:
            ms = jnp.mean(y * y, axis=-1, keepdims=True)
            y = y * lax.rsqrt(ms + EPS_RMS) * fg_ref[...]
        o_ref[...] = y


def _ffn(x, gain, shift, scale, gate, w_gu, w_dn, seq, final_gain=None, tm=512, tf=1408):
    t, d = x.shape
    f = w_dn.shape[0]
    tm = min(tm, seq)
    nf = f // tf
    per_b = seq // tm
    final = final_gain is not None
    vec = lambda i, j: (0, 0)
    modspec = pl.BlockSpec((1, 1, d), lambda i, j: (i // per_b, 0, 0))
    in_specs = [
        pl.BlockSpec((tm, d), lambda i, j: (i, 0)),
        pl.BlockSpec((1, d), vec),
        modspec, modspec, modspec,
        pl.BlockSpec((d, tf), lambda i, j: (0, j)),
        pl.BlockSpec((d, tf), lambda i, j: (0, j + nf)),
        pl.BlockSpec((tf, d), lambda i, j: (j, 0)),
    ]
    args = [x, gain.reshape(1, d), shift, scale, gate, w_gu, w_gu, w_dn]
    if final:
        in_specs.append(pl.BlockSpec((1, d), vec))
        args.append(final_gain.reshape(1, d))
    return pl.pallas_call(
        functools.partial(_ffn_kernel, final=final),
        grid=(t // tm, nf),
        in_specs=in_specs,
        out_specs=pl.BlockSpec((tm, d), lambda i, j: (i, 0)),
        out_shape=jax.ShapeDtypeStruct((t, d), F32),
        scratch_shapes=[pltpu.VMEM((tm, d), BF16), pltpu.VMEM((tm, d), F32)],
        compiler_params=pltpu.CompilerParams(
            dimension_semantics=("parallel", "arbitrary"), vmem_limit_bytes=VMEM_LIMIT),
        name="ffn_final" if final else "ffn",
    )(*args)


def _inproj_kernel(x_ref, gain_ref, sh_ref, sc_ref, *rest):
    n = len(rest) // 2
    w_refs, o_refs = rest[:n], rest[n:]
    h = _norm_mod(x_ref[...], gain_ref[...], sh_ref[0], sc_ref[0]).astype(BF16)
    for w_ref, o_ref in zip(w_refs, o_refs):
        o_ref[...] = _dot(h, w_ref[...])


def _inproj(x, gain, shift, scale, weights, seq, tm=512):
    t, d = x.shape
    tm = min(tm, seq)
    per_b = seq // tm
    modspec = pl.BlockSpec((1, 1, d), lambda i: (i // per_b, 0, 0))
    in_specs = [pl.BlockSpec((tm, d), lambda i: (i, 0)), pl.BlockSpec((1, d), lambda i: (0, 0)),
                modspec, modspec]
    in_specs += [pl.BlockSpec(w.shape, lambda i: (0, 0)) for w in weights]
    out_specs = [pl.BlockSpec((tm, w.shape[1]), lambda i: (i, 0)) for w in weights]
    out_shape = [jax.ShapeDtypeStruct((t, w.shape[1]), F32) for w in weights]
    return pl.pallas_call(
        _inproj_kernel,
        grid=(t // tm,),
        in_specs=in_specs,
        out_specs=out_specs,
        out_shape=out_shape,
        compiler_params=pltpu.CompilerParams(
            dimension_semantics=("parallel",), vmem_limit_bytes=VMEM_LIMIT),
        name="inproj",
    )(x, gain.reshape(1, d), shift, scale, *weights)


def _shift_rows(y, carry_row):
    rolled = pltpu.roll(y, 1, 0)
    first = lax.broadcasted_iota(jnp.int32, y.shape, 0) == 0
    return jnp.where(first, carry_row, rolled)


def _rwkv_kernel(*refs, has_vres, tc):
    if has_vres:
        (y_ref, yv_ref, vf_ref, mu_ref, w0_ref, wup_ref, a0_ref, aup_ref, gup_ref, kk_ref, ka_ref,
         rk_ref, gnw_ref, gnb_ref, vmu_ref, vup_ref, v0_ref,
         o_ref, state, carry, carry_v, o_scr) = refs
    else:
        (y_ref, mu_ref, w0_ref, wup_ref, a0_ref, aup_ref, gup_ref, kk_ref, ka_ref,
         rk_ref, gnw_ref, gnb_ref,
         o_ref, v_ref, state, carry, o_scr) = refs
    n, hd, w = RWKV_HEAD_DIM, RWKV_HEADS, RWKV_WIDTH

    @pl.when(pl.program_id(1) == 0)
    def _():
        state[...] = jnp.zeros_like(state)
        carry[...] = jnp.zeros_like(carry)
        if has_vres:
            carry_v[...] = jnp.zeros_like(carry_v)

    y = y_ref[...]
    prev = _shift_rows(y, carry[0:1, :])
    carry[0:1, :] = y[tc - 1:tc, :]
    ym = y + (prev - y) * mu_ref[...]
    r = ym[:, 0:w]
    k = ym[:, w:2 * w]
    v = ym[:, 2 * w:3 * w]
    wd = ym[:, 3 * w:3 * w + DECAY_RANK]
    ad = ym[:, 3 * w + DECAY_RANK:3 * w + DECAY_RANK + ICLR_RANK]
    gd = ym[:, 3 * w + DECAY_RANK + ICLR_RANK:]

    if has_vres:
        yv = yv_ref[...]
        prev_v = _shift_rows(yv, carry_v[0:1, :])
        carry_v[0:1, :] = yv[tc - 1:tc, :]
        yvm = yv + (prev_v - yv) * vmu_ref[...]
        logit = v0_ref[...] + _dot(yvm.astype(BF16), vup_ref[...].astype(BF16))
        v = v + (vf_ref[...] - v) * _sigmoid(logit)
    else:
        v_ref[...] = v

    w_log = -_softplus(-(w0_ref[...] + _dot(jnp.tanh(wd).astype(BF16), wup_ref[...].astype(BF16)))) - 0.5
    logw = -jnp.exp(w_log)
    a = _sigmoid(a0_ref[...] + _dot(ad.astype(BF16), aup_ref[...].astype(BF16)))
    g = _dot(_sigmoid(gd).astype(BF16), gup_ref[...].astype(BF16))

    hr = lax.broadcasted_iota(jnp.int32, (w, w), 0) // n
    hc = lax.broadcasted_iota(jnp.int32, (w, w), 1) // n
    head_ones = (hr == hc).astype(F32)

    kkr = k * kk_ref[...]
    kk = kkr * lax.rsqrt(_dot(kkr * kkr, head_ones, HI) + EPS_L2)
    k2 = k * (1.0 + (a - 1.0) * ka_ref[...])
    b = kk * a
    av = -kk

    incl = _tri(CHUNK, False)
    strict = _tri(CHUNK, True)
    l_incl = incl.astype(F32)

    for c in range(tc // CHUNK):
        rows = slice(c * CHUNK, (c + 1) * CHUNK)
        lw = logw[rows]
        cum = _dot(l_incl, lw, HI)
        last = cum[CHUNK - 1:CHUNK, :]
        e_pos = jnp.exp(cum)
        e_neg = jnp.exp(-cum)
        e_to_end = jnp.exp(last - cum)
        gam_end = jnp.exp(last)
        at = av[rows] * jnp.exp(cum - lw)
        rt = r[rows] * e_pos
        bt = b[rows] * e_neg
        kt = k2[rows] * e_neg
        bg = b[rows] * e_to_end
        kg = k2[rows] * e_to_end
        vc = v[rows]
        for h in range(hd):
            ln = slice(h * n, (h + 1) * n)
            x_ar = jnp.concatenate([at[:, ln], rt[:, ln]], axis=0)
            y_bk = jnp.concatenate([bt[:, ln], kt[:, ln]], axis=0)
            m1 = _dg(x_ar, y_bk, NT, HI)
            a_ab = jnp.where(strict, m1[:CHUNK, :CHUNK], 0.0)
            a_ak = jnp.where(strict, m1[:CHUNK, CHUNK:], 0.0)
            a_rb = jnp.where(incl, m1[CHUNK:, :CHUNK], 0.0)
            a_rk = jnp.where(incl, m1[CHUNK:, CHUNK:], 0.0)
            s0 = state[h]
            p = _dg(x_ar, s0, NT, HI)
            vh = vc[:, ln]
            u = _unit_lower_solve(a_ab, p[:CHUNK] + _dot(a_ak, vh, HI))
            uv = jnp.concatenate([u, vh], axis=0)
            o_h = p[CHUNK:] + _dot(jnp.concatenate([a_rb, a_rk], axis=1), uv, HI)
            bk_end = jnp.concatenate([bg[:, ln], kg[:, ln]], axis=0)
            state[h] = s0 * gam_end[:, ln] + _dg(uv, bk_end, TN, HI)
            o_scr[rows, ln] = o_h

    o = o_scr[...]
    inv_n = 1.0 / n
    mean = _dot(o, head_ones, HI) * inv_n
    dev = o - mean
    var = _dot(dev * dev, head_ones, HI) * inv_n
    on = dev * lax.rsqrt(var + EPS_GN) * gnw_ref[...] + gnb_ref[...]
    bonus = _dot(r * k2 * rk_ref[...], head_ones, HI)
    o_ref[...] = (on + bonus * v) * g


def _rwkv(y_r, params, seq, y_vres=None, v_first=None, vres_params=None, tc=256):
    t = y_r.shape[0]
    tc = min(tc, seq)
    per_b = seq // tc
    nb = t // seq
    has_vres = y_vres is not None
    w = RWKV_WIDTH
    row = lambda b, s: (b * per_b + s, 0)
    full = lambda b, s: (0, 0)
    in_specs = [pl.BlockSpec((tc, RWKV_COLS), row)]
    args = [y_r]
    if has_vres:
        in_specs += [pl.BlockSpec((tc, LANE), row), pl.BlockSpec((tc, w), row)]
        args += [y_vres, v_first]
    plist = list(params) + (list(vres_params) if has_vres else [])
    in_specs += [pl.BlockSpec(p.shape, full) for p in plist]
    args += plist
    out_specs = [pl.BlockSpec((tc, w), row)]
    out_shape = [jax.ShapeDtypeStruct((t, w), F32)]
    scratch = [pltpu.VMEM((RWKV_HEADS, RWKV_HEAD_DIM, RWKV_HEAD_DIM), F32),
               pltpu.VMEM((SUBLANE, RWKV_COLS), F32)]
    if has_vres:
        scratch.append(pltpu.VMEM((SUBLANE, LANE), F32))
    else:
        out_specs.append(pl.BlockSpec((tc, w), row))
        out_shape.append(jax.ShapeDtypeStruct((t, w), F32))
    scratch.append(pltpu.VMEM((tc, w), F32))
    return pl.pallas_call(
        functools.partial(_rwkv_kernel, has_vres=has_vres, tc=tc),
        grid=(nb, per_b),
        in_specs=in_specs,
        out_specs=out_specs,
        out_shape=out_shape,
        scratch_shapes=scratch,
        compiler_params=pltpu.CompilerParams(
            dimension_semantics=("parallel", "arbitrary"), vmem_limit_bytes=VMEM_LIMIT),
        name="rwkv7_vres" if has_vres else "rwkv7",
    )(*args)


def _gdn_kernel(qkv_ref, z_ref, ba_ref, cw_ref, alog_r_ref, dt_r_ref, alog_c_ref, dt_c_ref, nw_ref,
                o_ref, state, ext, o_scr, *, tc):
    dh, nh, w = GDN_HEAD_DIM, GDN_HEADS, GDN_WIDTH
    pad = SUBLANE

    @pl.when(pl.program_id(1) == 0)
    def _():
        state[...] = jnp.zeros_like(state)
        ext[0:pad, :] = jnp.zeros((pad, GDN_QKV), F32)

    x = qkv_ref[...]
    ext[pad:pad + tc, :] = x
    conv = x * cw_ref[CONV_WIDTH - 1:CONV_WIDTH, :]
    for j in range(CONV_WIDTH - 1):
        back = CONV_WIDTH - 1 - j
        conv = conv + ext[pad - back:pad - back + tc, :] * cw_ref[j:j + 1, :]
    ext[0:pad, :] = x[tc - pad:tc, :]
    qkv = _silu(conv)

    ba = ba_ref[...]
    beta_c = _sigmoid(ba)
    g_c = -jnp.exp(alog_r_ref[...]) * _softplus(ba + dt_r_ref[...])
    er = lax.broadcasted_iota(jnp.int32, (pad, LANE), 0)
    ec = lax.broadcasted_iota(jnp.int32, (pad, LANE), 1)
    ba_t = _dg((er == ec).astype(F32), ba, NT, HI)
    g_r = -jnp.exp(alog_c_ref[...]) * _softplus(ba_t + dt_c_ref[...])

    incl = _tri(CHUNK, False)
    strict = _tri(CHUNK, True)
    l_incl = incl.astype(F32)
    u_incl = (lax.broadcasted_iota(jnp.int32, (CHUNK, CHUNK), 0)
              <= lax.broadcasted_iota(jnp.int32, (CHUNK, CHUNK), 1)).astype(F32)

    for c in range(tc // CHUNK):
        rows = slice(c * CHUNK, (c + 1) * CHUNK)
        cum_c = _dot(l_incl, g_c[rows], HI)
        cum_r = _dot(g_r[:, rows], u_incl, HI)
        for h in range(nh):
            ccol = cum_c[:, nh + h:nh + h + 1]
            crow = cum_r[nh + h:nh + h + 1, :]
            clast = ccol[CHUNK - 1:CHUNK, :]
            decay = jnp.where(incl, jnp.exp(jnp.where(incl, ccol - crow, 0.0)), 0.0)
            beta = beta_c[rows, h:h + 1]
            qh = qkv[rows, h * dh:(h + 1) * dh]
            kh = qkv[rows, w + h * dh:w + (h + 1) * dh]
            vh = qkv[rows, 2 * w + h * dh:2 * w + (h + 1) * dh]
            qh = qh * lax.rsqrt(jnp.sum(qh * qh, axis=-1, keepdims=True) + EPS_L2) * (dh ** -0.5)
            kh = kh * lax.rsqrt(jnp.sum(kh * kh, axis=-1, keepdims=True) + EPS_L2)
            kkm = _dg(kh, kh, NT, HI)
            neg_n = -jnp.where(strict, kkm * decay * beta, 0.0)
            e_c = jnp.exp(ccol)
            rhs = jnp.concatenate([vh * beta, kh * (beta * e_c)], axis=1)
            sol = _unit_lower_solve(neg_n, rhs)
            u = sol[:, :dh]
            wmat = sol[:, dh:]
            p = _dg(qh, kh, NT, HI) * decay
            q_dec = qh * e_c
            k_dec = kh * jnp.exp(clast - ccol)
            st = state[h]
            v_new = u - _dot(wmat, st, HI)
            o_h = _dot(q_dec, st, HI) + _dot(p, v_new, HI)
            state[h] = st * jnp.exp(clast) + _dg(k_dec, v_new, TN, HI)
            o_scr[rows, h * dh:(h + 1) * dh] = o_h

    zs = _silu(z_ref[...])
    for h in range(nh):
        ln = slice(h * dh, (h + 1) * dh)
        o = o_scr[:, ln]
        o = o * lax.rsqrt(jnp.mean(o * o, axis=-1, keepdims=True) + EPS_RMS) * nw_ref[...]
        o_ref[:, ln] = o * zs[:, ln]


def _gdn(y_qkv, y_z, y_ba, params, seq, tc=256):
    t = y_qkv.shape[0]
    tc = min(tc, seq)
    per_b = seq // tc
    nb = t // seq
    w = GDN_WIDTH
    row = lambda b, s: (b * per_b + s, 0)
    full = lambda b, s: (0, 0)
    in_specs = [pl.BlockSpec((tc, GDN_QKV), row), pl.BlockSpec((tc, w), row), pl.BlockSpec((tc, LANE), row)]
    in_specs += [pl.BlockSpec(p.shape, full) for p in params]
    return pl.pallas_call(
        functools.partial(_gdn_kernel, tc=tc),
        grid=(nb, per_b),
        in_specs=in_specs,
        out_specs=pl.BlockSpec((tc, w), row),
        out_shape=jax.ShapeDtypeStruct((t, w), F32),
        scratch_shapes=[pltpu.VMEM((GDN_HEADS, GDN_HEAD_DIM, GDN_HEAD_DIM), F32),
                        pltpu.VMEM((tc + SUBLANE, GDN_QKV), F32),
                        pltpu.VMEM((tc, w), F32)],
        compiler_params=pltpu.CompilerParams(
            dimension_semantics=("parallel", "arbitrary"), vmem_limit_bytes=VMEM_LIMIT),
        name="gdn",
    )(y_qkv, y_z, y_ba, *params)


def _outproj_kernel(x_ref, gt_ref, or_ref, og_ref, w1_ref, w2_ref, o_ref):
    mixed = _dot(or_ref[...].astype(BF16), w1_ref[...]) + _dot(og_ref[...].astype(BF16), w2_ref[...])
    o_ref[...] = x_ref[...] + gt_ref[0] * mixed


def _outproj(x, gate, o_r, o_g, w1, w2, seq, tm=512):
    t, d = x.shape
    tm = min(tm, seq)
    per_b = seq // tm
    row = lambda i: (i, 0)
    full = lambda i: (0, 0)
    return pl.pallas_call(
        _outproj_kernel,
        grid=(t // tm,),
        in_specs=[pl.BlockSpec((tm, d), row),
                  pl.BlockSpec((1, 1, d), lambda i: (i // per_b, 0, 0)),
                  pl.BlockSpec((tm, o_r.shape[1]), row), pl.BlockSpec((tm, o_g.shape[1]), row),
                  pl.BlockSpec(w1.shape, full), pl.BlockSpec(w2.shape, full)],
        out_specs=pl.BlockSpec((tm, d), row),
        out_shape=jax.ShapeDtypeStruct((t, d), F32),
        compiler_params=pltpu.CompilerParams(
            dimension_semantics=("parallel",), vmem_limit_bytes=VMEM_LIMIT),
        name="outproj",
    )(x, gate, o_r, o_g, w1, w2)


def _pad_cols(m, width):
    return jnp.pad(m, ((0, 0), (0, width - m.shape[1])))


def kernel(x, c, norm_gain, ada_w, ada_b, ffn_w_gu, ffn_w_down, w_in, w_out, rwkv_mu, rwkv_w0, rwkv_w_up, rwkv_a0, rwkv_a_up, rwkv_g_up, rwkv_k_k, rwkv_k_a, rwkv_r_k, rwkv_gn_w, rwkv_gn_b, vres_w_down, vres_mu, vres_w_up, vres_v0, gdn_conv_w, gdn_a_log, gdn_dt_bias, gdn_norm_w, final_gain):
    bsz, seq, d = x.shape
    depth = norm_gain.shape[0]
    nh = GDN_HEADS
    mod = _adaln(c, ada_w, ada_b)
    xf = x.reshape(bsz * seq, d)
    v_first = None
    row = lambda p: p.reshape(1, -1)
    for l in range(depth):
        def mods(sub):
            m = mod[l * 3 + sub]
            return tuple(m[:, i * d:(i + 1) * d].reshape(bsz, 1, d) for i in range(3))

        shift, scale, gate = mods(0)
        xf = _ffn(xf, norm_gain[l, 0], shift, scale, gate,
                  ffn_w_gu[l, 0].astype(BF16), ffn_w_down[l, 0].astype(BF16), seq)

        shift, scale, gate = mods(1)
        wl = w_in[l]
        w_ba = _pad_cols(wl[:, RWKV_COLS + 4 * GDN_WIDTH:], LANE)
        weights = [wl[:, :RWKV_COLS], wl[:, RWKV_COLS:RWKV_COLS + GDN_QKV],
                   wl[:, RWKV_COLS + GDN_QKV:RWKV_COLS + 4 * GDN_WIDTH], w_ba]
        if l > 0:
            weights.append(_pad_cols(vres_w_down[l - 1], LANE))
        ys = _inproj(xf, norm_gain[l, 1], shift, scale, [wt.astype(BF16) for wt in weights], seq)
        y_r, y_qkv, y_z, y_ba = ys[:4]

        rparams = [row(rwkv_mu[l]), row(rwkv_w0[l]), rwkv_w_up[l], row(rwkv_a0[l]), rwkv_a_up[l],
                   rwkv_g_up[l], row(rwkv_k_k[l]), row(rwkv_k_a[l]), row(rwkv_r_k[l]),
                   row(rwkv_gn_w[l]), row(rwkv_gn_b[l])]
        if l == 0:
            o_r, v_first = _rwkv(y_r, rparams, seq)
        else:
            vparams = [_pad_cols(row(vres_mu[l - 1]), LANE),
                       jnp.pad(vres_w_up[l - 1], ((0, LANE - VRES_RANK), (0, 0))),
                       row(vres_v0[l - 1])]
            (o_r,) = _rwkv(y_r, rparams, seq, y_vres=ys[4], v_first=v_first, vres_params=vparams)

        alog = gdn_a_log[l].astype(F32)
        dtb = gdn_dt_bias[l].astype(F32)
        place = lambda p: jnp.pad(p, (nh, LANE - 2 * nh))
        gparams = [gdn_conv_w[l], row(place(alog)), row(place(dtb)),
                   place(alog)[:SUBLANE].reshape(SUBLANE, 1), place(dtb)[:SUBLANE].reshape(SUBLANE, 1),
                   row(gdn_norm_w[l])]
        o_g = _gdn(y_qkv, y_z, y_ba, gparams, seq)

        wo = w_out[l].astype(BF16)
        xf = _outproj(xf, gate, o_r, o_g, wo[:RWKV_WIDTH], wo[RWKV_WIDTH:], seq)

        shift, scale, gate = mods(2)
        xf = _ffn(xf, norm_gain[l, 2], shift, scale, gate,
                  ffn_w_gu[l, 1].astype(BF16), ffn_w_down[l, 1].astype(BF16), seq,
                  final_gain=final_gain if l == depth - 1 else None)
    return xf.reshape(bsz, seq, d)
```

```python
import functools

import jax
import jax.numpy as jnp
from jax import lax
from jax.experimental import pallas as pl
from jax.experimental.pallas import tpu as pltpu

F32 = jnp.float32
BF16 = jnp.bfloat16
HI = lax.Precision.HIGHEST

RWKV_HEAD_DIM = 64
RWKV_HEADS = 8
RWKV_WIDTH = RWKV_HEAD_DIM * RWKV_HEADS
GDN_HEAD_DIM = 128
GDN_HEADS = 4
GDN_WIDTH = GDN_HEAD_DIM * GDN_HEADS
DECAY_RANK = 64
ICLR_RANK = 64
VRES_RANK = 32
GATE_RANK = 128
RWKV_COLS = 3 * RWKV_WIDTH + DECAY_RANK + ICLR_RANK + GATE_RANK
GDN_QKV = 3 * GDN_WIDTH
CONV_WIDTH = 4
CHUNK = 64
LANE = 128
SUBLANE = 8
EPS_RMS = 1e-6
EPS_GN = 64e-5
EPS_L2 = 1e-6
MACARON_WEIGHT = 0.5
VMEM_LIMIT = 52 * 1024 * 1024

NT = (((1,), (1,)), ((), ()))
TN = (((0,), (0,)), ((), ()))


def _sigmoid(x):
    return 1.0 / (1.0 + jnp.exp(-x))


def _softplus(x):
    return jnp.maximum(x, 0.0) + jnp.log(1.0 + jnp.exp(-jnp.abs(x)))


def _silu(x):
    return x * _sigmoid(x)


def _dot(a, b, precision=None):
    return jnp.dot(a, b, preferred_element_type=F32, precision=precision)


def _dg(a, b, dims, precision=None):
    return lax.dot_general(a, b, dims, preferred_element_type=F32, precision=precision)


def _bdot(a, b):
    return jnp.dot(a.astype(BF16), b.astype(BF16), preferred_element_type=F32)


def _bdg(a, b, dims):
    return lax.dot_general(a.astype(BF16), b.astype(BF16), dims, preferred_element_type=F32)


def _split2(a):
    hi = a.astype(BF16)
    return hi, (a - hi.astype(F32)).astype(BF16)


def _dot3(a, b):
    ah, al = _split2(a)
    bh, bl = _split2(b)
    d = lambda x, y: jnp.dot(x, y, preferred_element_type=F32)
    return d(ah, bh) + (d(ah, bl) + d(al, bh))


def _norm_mod(x, gain, shift, scale):
    ms = jnp.mean(x * x, axis=-1, keepdims=True)
    y = x * lax.rsqrt(ms + EPS_RMS) * gain
    return y * (1.0 + scale) + shift


def _tri(n, strict):
    r = lax.broadcasted_iota(jnp.int32, (n, n), 0)
    c = lax.broadcasted_iota(jnp.int32, (n, n), 1)
    return (r > c) if strict else (r >= c)


def _unit_lower_solve(neg_strict, rhs, dot):
    n = neg_strict
    x = rhs
    steps = CHUNK.bit_length() - 1
    for i in range(steps):
        x = x + dot(n, x)
        if i + 1 < steps:
            n = dot(n, n)
    return x


def _mod_kernel(c_ref, w_ref, b_ref, o_ref):
    ca = _silu(c_ref[...])
    o_ref[0] = _dot(ca, w_ref[0], HI) + b_ref[0]


def _adaln(c, ada_w, ada_b):
    n_l, n_sub, d, d3 = ada_w.shape
    b = c.shape[0]
    rows = -(-b // SUBLANE) * SUBLANE
    c_pad = jnp.pad(c, ((0, rows - b), (0, 0)))
    tn = d
    out = pl.pallas_call(
        _mod_kernel,
        grid=(n_l * n_sub, d3 // tn),
        in_specs=[
            pl.BlockSpec((rows, d), lambda i, j: (0, 0)),
            pl.BlockSpec((1, d, tn), lambda i, j: (i, 0, j)),
            pl.BlockSpec((1, 1, tn), lambda i, j: (i, 0, j)),
        ],
        out_specs=pl.BlockSpec((1, rows, tn), lambda i, j: (i, 0, j)),
        out_shape=jax.ShapeDtypeStruct((n_l * n_sub, rows, d3), F32),
        compiler_params=pltpu.CompilerParams(
            dimension_semantics=("parallel", "parallel"), vmem_limit_bytes=VMEM_LIMIT),
        name="adaln_mod",
    )(c_pad, ada_w.reshape(n_l * n_sub, d, d3), ada_b.reshape(n_l * n_sub, 1, d3))
    return out[:, :b, :]


def _ffn_kernel(x_ref, gain_ref, sh_ref, sc_ref, gt_ref, wg_ref, wu_ref, wd_ref, *rest, final):
    if final:
        fg_ref, o_ref, h_scr, acc = rest
    else:
        o_ref, h_scr, acc = rest
    j = pl.program_id(1)

    @pl.when(j == 0)
    def _():
        h = _norm_mod(x_ref[...], gain_ref[...], sh_ref[0], sc_ref[0])
        h_scr[...] = h.astype(BF16)
        acc[...] = jnp.zeros_like(acc)

    h = h_scr[...]
    g = _dot(h, wg_ref[...])
    u = _dot(h, wu_ref[...])
    act = (_silu(g) * u).astype(BF16)
    acc[...] += _dot(act, wd_ref[...])

    @pl.when(j == pl.num_programs(1) - 1)
    def _():
        y = x_ref[...] + (MACARON_WEIGHT * gt_ref[0]) * acc[...]
        if final:
            ms = jnp.mean(y * y, axis=-1, keepdims=True)
            y = y * lax.rsqrt(ms + EPS_RMS) * fg_ref[...]
        o_ref[...] = y


def _ffn(x, gain, shift, scale, gate, w_gu, w_dn, seq, final_gain=None, tm=512, tf=1408):
    t, d = x.shape
    f = w_dn.shape[0]
    tm = min(tm, seq)
    nf = f // tf
    per_b = seq // tm
    final = final_gain is not None
    vec = lambda i, j: (0, 0)
    modspec = pl.BlockSpec((1, 1, d), lambda i, j: (i // per_b, 0, 0))
    in_specs = [
        pl.BlockSpec((tm, d), lambda i, j: (i, 0)),
        pl.BlockSpec((1, d), vec),
        modspec, modspec, modspec,
        pl.BlockSpec((d, tf), lambda i, j: (0, j)),
        pl.BlockSpec((d, tf), lambda i, j: (0, j + nf)),
        pl.BlockSpec((tf, d), lambda i, j: (j, 0)),
    ]
    args = [x, gain.reshape(1, d), shift, scale, gate, w_gu, w_gu, w_dn]
    if final:
        in_specs.append(pl.BlockSpec((1, d), vec))
        args.append(final_gain.reshape(1, d))
    return pl.pallas_call(
        functools.partial(_ffn_kernel, final=final),
        grid=(t // tm, nf),
        in_specs=in_specs,
        out_specs=pl.BlockSpec((tm, d), lambda i, j: (i, 0)),
        out_shape=jax.ShapeDtypeStruct((t, d), F32),
        scratch_shapes=[pltpu.VMEM((tm, d), BF16), pltpu.VMEM((tm, d), F32)],
        compiler_params=pltpu.CompilerParams(
            dimension_semantics=("parallel", "arbitrary"), vmem_limit_bytes=VMEM_LIMIT),
        name="ffn_final" if final else "ffn",
    )(*args)


def _inproj_kernel(x_ref, gain_ref, sh_ref, sc_ref, *rest):
    n = len(rest) // 2
    w_refs, o_refs = rest[:n], rest[n:]
    h = _norm_mod(x_ref[...], gain_ref[...], sh_ref[0], sc_ref[0]).astype(BF16)
    for w_ref, o_ref in zip(w_refs, o_refs):
        o_ref[...] = _dot(h, w_ref[...])


def _inproj(x, gain, shift, scale, weights, seq, tm=512):
    t, d = x.shape
    tm = min(tm, seq)
    per_b = seq // tm
    modspec = pl.BlockSpec((1, 1, d), lambda i: (i // per_b, 0, 0))
    in_specs = [pl.BlockSpec((tm, d), lambda i: (i, 0)), pl.BlockSpec((1, d), lambda i: (0, 0)),
                modspec, modspec]
    in_specs += [pl.BlockSpec(w.shape, lambda i: (0, 0)) for w in weights]
    out_specs = [pl.BlockSpec((tm, w.shape[1]), lambda i: (i, 0)) for w in weights]
    out_shape = [jax.ShapeDtypeStruct((t, w.shape[1]), F32) for w in weights]
    return pl.pallas_call(
        _inproj_kernel,
        grid=(t // tm,),
        in_specs=in_specs,
        out_specs=out_specs,
        out_shape=out_shape,
        compiler_params=pltpu.CompilerParams(
            dimension_semantics=("parallel",), vmem_limit_bytes=VMEM_LIMIT),
        name="inproj",
    )(x, gain.reshape(1, d), shift, scale, *weights)


def _shift_rows(y, carry_row):
    rolled = pltpu.roll(y, 1, 0)
    first = lax.broadcasted_iota(jnp.int32, y.shape, 0) == 0
    return jnp.where(first, carry_row, rolled)


def _rwkv_kernel(*refs, has_vres, tc):
    if has_vres:
        (y_ref, yv_ref, vf_ref, mu_ref, w0_ref, wup_ref, a0_ref, aup_ref, gup_ref, kk_ref, ka_ref,
         rk_ref, gnw_ref, gnb_ref, vmu_ref, vup_ref, v0_ref,
         o_ref, state, carry, carry_v, o_scr) = refs
    else:
        (y_ref, mu_ref, w0_ref, wup_ref, a0_ref, aup_ref, gup_ref, kk_ref, ka_ref,
         rk_ref, gnw_ref, gnb_ref,
         o_ref, v_ref, state, carry, o_scr) = refs
    n, hd, w = RWKV_HEAD_DIM, RWKV_HEADS, RWKV_WIDTH

    @pl.when(pl.program_id(1) == 0)
    def _():
        state[...] = jnp.zeros_like(state)
        carry[...] = jnp.zeros_like(carry)
        if has_vres:
            carry_v[...] = jnp.zeros_like(carry_v)

    y = y_ref[...]
    prev = _shift_rows(y, carry[0:1, :])
    carry[0:1, :] = y[tc - 1:tc, :]
    ym = y + (prev - y) * mu_ref[...]
    r = ym[:, 0:w]
    k = ym[:, w:2 * w]
    v = ym[:, 2 * w:3 * w]
    wd = ym[:, 3 * w:3 * w + DECAY_RANK]
    ad = ym[:, 3 * w + DECAY_RANK:3 * w + DECAY_RANK + ICLR_RANK]
    gd = ym[:, 3 * w + DECAY_RANK + ICLR_RANK:]

    if has_vres:
        yv = yv_ref[...]
        prev_v = _shift_rows(yv, carry_v[0:1, :])
        carry_v[0:1, :] = yv[tc - 1:tc, :]
        yvm = yv + (prev_v - yv) * vmu_ref[...]
        logit = v0_ref[...] + _dot(yvm.astype(BF16), vup_ref[...].astype(BF16))
        v = v + (vf_ref[...] - v) * _sigmoid(logit)
    else:
        v_ref[...] = v

    w_log = -_softplus(-(w0_ref[...] + _dot(jnp.tanh(wd).astype(BF16), wup_ref[...].astype(BF16)))) - 0.5
    logw = -jnp.exp(w_log)
    a = _sigmoid(a0_ref[...] + _dot(ad.astype(BF16), aup_ref[...].astype(BF16)))
    g = _dot(_sigmoid(gd).astype(BF16), gup_ref[...].astype(BF16))

    hr = lax.broadcasted_iota(jnp.int32, (w, w), 0) // n
    hc = lax.broadcasted_iota(jnp.int32, (w, w), 1) // n
    head_ones = (hr == hc).astype(F32)

    kkr = k * kk_ref[...]
    kk = kkr * lax.rsqrt(_dot(kkr * kkr, head_ones, HI) + EPS_L2)
    k2 = k * (1.0 + (a - 1.0) * ka_ref[...])
    b = kk * a
    av = -kk

    incl = _tri(CHUNK, False)
    strict = _tri(CHUNK, True)
    l_incl = incl.astype(F32)

    for c in range(tc // CHUNK):
        rows = slice(c * CHUNK, (c + 1) * CHUNK)
        lw = logw[rows]
        cum = _dot(l_incl, lw, HI)
        last = cum[CHUNK - 1:CHUNK, :]
        e_pos = jnp.exp(cum)
        e_neg = jnp.exp(-cum)
        e_to_end = jnp.exp(last - cum)
        gam_end = jnp.exp(last)
        at = av[rows] * jnp.exp(cum - lw)
        rt = r[rows] * e_pos
        bt = b[rows] * e_neg
        kt = k2[rows] * e_neg
        bg = b[rows] * e_to_end
        kg = k2[rows] * e_to_end
        vc = v[rows]
        for h in range(hd):
            ln = slice(h * n, (h + 1) * n)
            x_ar = jnp.concatenate([at[:, ln], rt[:, ln]], axis=0)
            y_bk = jnp.concatenate([bt[:, ln], kt[:, ln]], axis=0)
            m1 = _bdg(x_ar, y_bk, NT)
            a_ab = jnp.where(strict, m1[:CHUNK, :CHUNK], 0.0)
            a_ak = jnp.where(strict, m1[:CHUNK, CHUNK:], 0.0)
            a_rb = jnp.where(incl, m1[CHUNK:, :CHUNK], 0.0)
            a_rk = jnp.where(incl, m1[CHUNK:, CHUNK:], 0.0)
            s0 = state[h]
            p = _bdg(x_ar, s0, NT)
            vh = vc[:, ln]
            u = _unit_lower_solve(a_ab, p[:CHUNK] + _bdot(a_ak, vh), _bdot)
            uv = jnp.concatenate([u, vh], axis=0)
            o_h = p[CHUNK:] + _bdot(jnp.concatenate([a_rb, a_rk], axis=1), uv)
            bk_end = jnp.concatenate([bg[:, ln], kg[:, ln]], axis=0)
            state[h] = s0 * gam_end[:, ln] + _bdg(uv, bk_end, TN)
            o_scr[rows, ln] = o_h

    o = o_scr[...]
    inv_n = 1.0 / n
    mean = _dot(o, head_ones, HI) * inv_n
    dev = o - mean
    var = _dot(dev * dev, head_ones, HI) * inv_n
    on = dev * lax.rsqrt(var + EPS_GN) * gnw_ref[...] + gnb_ref[...]
    bonus = _dot(r * k2 * rk_ref[...], head_ones, HI)
    o_ref[...] = (on + bonus * v) * g


def _rwkv(y_r, params, seq, y_vres=None, v_first=None, vres_params=None, tc=256):
    t = y_r.shape[0]
    tc = min(tc, seq)
    per_b = seq // tc
    nb = t // seq
    has_vres = y_vres is not None
    w = RWKV_WIDTH
    row = lambda b, s: (b * per_b + s, 0)
    full = lambda b, s: (0, 0)
    in_specs = [pl.BlockSpec((tc, RWKV_COLS), row)]
    args = [y_r]
    if has_vres:
        in_specs += [pl.BlockSpec((tc, LANE), row), pl.BlockSpec((tc, w), row)]
        args += [y_vres, v_first]
    plist = list(params) + (list(vres_params) if has_vres else [])
    in_specs += [pl.BlockSpec(p.shape, full) for p in plist]
    args += plist
    out_specs = [pl.BlockSpec((tc, w), row)]
    out_shape = [jax.ShapeDtypeStruct((t, w), F32)]
    scratch = [pltpu.VMEM((RWKV_HEADS, RWKV_HEAD_DIM, RWKV_HEAD_DIM), F32),
               pltpu.VMEM((SUBLANE, RWKV_COLS), F32)]
    if has_vres:
        scratch.append(pltpu.VMEM((SUBLANE, LANE), F32))
    else:
        out_specs.append(pl.BlockSpec((tc, w), row))
        out_shape.append(jax.ShapeDtypeStruct((t, w), F32))
    scratch.append(pltpu.VMEM((tc, w), F32))
    return pl.pallas_call(
        functools.partial(_rwkv_kernel, has_vres=has_vres, tc=tc),
        grid=(nb, per_b),
        in_specs=in_specs,
        out_specs=out_specs,
        out_shape=out_shape,
        scratch_shapes=scratch,
        compiler_params=pltpu.CompilerParams(
            dimension_semantics=("parallel", "arbitrary"), vmem_limit_bytes=VMEM_LIMIT),
        name="rwkv7_vres" if has_vres else "rwkv7",
    )(*args)


def _gdn_kernel(qkv_ref, z_ref, ba_ref, cw_ref, alog_r_ref, dt_r_ref, alog_c_ref, dt_c_ref, nw_ref,
                o_ref, state, ext, o_scr, *, tc):
    dh, nh, w = GDN_HEAD_DIM, GDN_HEADS, GDN_WIDTH
    pad = SUBLANE

    @pl.when(pl.program_id(1) == 0)
    def _():
        state[...] = jnp.zeros_like(state)
        ext[0:pad, :] = jnp.zeros((pad, GDN_QKV), F32)

    x = qkv_ref[...]
    ext[pad:pad + tc, :] = x
    conv = x * cw_ref[CONV_WIDTH - 1:CONV_WIDTH, :]
    for j in range(CONV_WIDTH - 1):
        back = CONV_WIDTH - 1 - j
        conv = conv + ext[pad - back:pad - back + tc, :] * cw_ref[j:j + 1, :]
    ext[0:pad, :] = x[tc - pad:tc, :]
    qkv = _silu(conv)

    ba = ba_ref[...]
    beta_c = _sigmoid(ba)
    g_c = -jnp.exp(alog_r_ref[...]) * _softplus(ba + dt_r_ref[...])
    er = lax.broadcasted_iota(jnp.int32, (pad, LANE), 0)
    ec = lax.broadcasted_iota(jnp.int32, (pad, LANE), 1)
    ba_t = _dg((er == ec).astype(F32), ba, NT, HI)
    g_r = -jnp.exp(alog_c_ref[...]) * _softplus(ba_t + dt_c_ref[...])

    incl = _tri(CHUNK, False)
    strict = _tri(CHUNK, True)
    l_incl = incl.astype(F32)
    u_incl = (lax.broadcasted_iota(jnp.int32, (CHUNK, CHUNK), 0)
              <= lax.broadcasted_iota(jnp.int32, (CHUNK, CHUNK), 1)).astype(F32)

    for c in range(tc // CHUNK):
        rows = slice(c * CHUNK, (c + 1) * CHUNK)
        cum_c = _dot(l_incl, g_c[rows], HI)
        cum_r = _dot(g_r[:, rows], u_incl, HI)
        for h in range(nh):
            ccol = cum_c[:, nh + h:nh + h + 1]
            crow = cum_r[nh + h:nh + h + 1, :]
            clast = ccol[CHUNK - 1:CHUNK, :]
            decay = jnp.where(incl, jnp.exp(jnp.where(incl, ccol - crow, 0.0)), 0.0)
            beta = beta_c[rows, h:h + 1]
            qh = qkv[rows, h * dh:(h + 1) * dh]
            kh = qkv[rows, w + h * dh:w + (h + 1) * dh]
            vh = qkv[rows, 2 * w + h * dh:2 * w + (h + 1) * dh]
            qh = qh * lax.rsqrt(jnp.sum(qh * qh, axis=-1, keepdims=True) + EPS_L2) * (dh ** -0.5)
            kh = kh * lax.rsqrt(jnp.sum(kh * kh, axis=-1, keepdims=True) + EPS_L2)
            kkm = _bdg(kh, kh, NT)
            neg_n = -jnp.where(strict, kkm * decay * beta, 0.0)
            e_c = jnp.exp(ccol)
            rhs = jnp.concatenate([vh * beta, kh * (beta * e_c)], axis=1)
            sol = _unit_lower_solve(neg_n, rhs, _dot3)
            u = sol[:, :dh]
            wmat = sol[:, dh:]
            p = _bdg(qh, kh, NT) * decay
            q_dec = qh * e_c
            k_dec = kh * jnp.exp(clast - ccol)
            st = state[h]
            v_new = u - _bdot(wmat, st)
            o_h = _bdot(q_dec, st) + _bdot(p, v_new)
            state[h] = st * jnp.exp(clast) + _bdg(k_dec, v_new, TN)
            o_scr[rows, h * dh:(h + 1) * dh] = o_h

    zs = _silu(z_ref[...])
    for h in range(nh):
        ln = slice(h * dh, (h + 1) * dh)
        o = o_scr[:, ln]
        o = o * lax.rsqrt(jnp.mean(o * o, axis=-1, keepdims=True) + EPS_RMS) * nw_ref[...]
        o_ref[:, ln] = o * zs[:, ln]


def _gdn(y_qkv, y_z, y_ba, params, seq, tc=256):
    t = y_qkv.shape[0]
    tc = min(tc, seq)
    per_b = seq // tc
    nb = t // seq
    w = GDN_WIDTH
    row = lambda b, s: (b * per_b + s, 0)
    full = lambda b, s: (0, 0)
    in_specs = [pl.BlockSpec((tc, GDN_QKV), row), pl.BlockSpec((tc, w), row), pl.BlockSpec((tc, LANE), row)]
    in_specs += [pl.BlockSpec(p.shape, full) for p in params]
    return pl.pallas_call(
        functools.partial(_gdn_kernel, tc=tc),
        grid=(nb, per_b),
        in_specs=in_specs,
        out_specs=pl.BlockSpec((tc, w), row),
        out_shape=jax.ShapeDtypeStruct((t, w), F32),
        scratch_shapes=[pltpu.VMEM((GDN_HEADS, GDN_HEAD_DIM, GDN_HEAD_DIM), F32),
                        pltpu.VMEM((tc + SUBLANE, GDN_QKV), F32),
                        pltpu.VMEM((tc, w), F32)],
        compiler_params=pltpu.CompilerParams(
            dimension_semantics=("parallel", "arbitrary"), vmem_limit_bytes=VMEM_LIMIT),
        name="gdn",
    )(y_qkv, y_z, y_ba, *params)


def _outproj_kernel(x_ref, gt_ref, or_ref, og_ref, w1_ref, w2_ref, o_ref):
    mixed = _dot(or_ref[...].astype(BF16), w1_ref[...]) + _dot(og_ref[...].astype(BF16), w2_ref[...])
    o_ref[...] = x_ref[...] + gt_ref[0] * mixed


def _outproj(x, gate, o_r, o_g, w1, w2, seq, tm=512):
    t, d = x.shape
    tm = min(tm, seq)
    per_b = seq // tm
    row = lambda i: (i, 0)
    full = lambda i: (0, 0)
    return pl.pallas_call(
        _outproj_kernel,
        grid=(t // tm,),
        in_specs=[pl.BlockSpec((tm, d), row),
                  pl.BlockSpec((1, 1, d), lambda i: (i // per_b, 0, 0)),
                  pl.BlockSpec((tm, o_r.shape[1]), row), pl.BlockSpec((tm, o_g.shape[1]), row),
                  pl.BlockSpec(w1.shape, full), pl.BlockSpec(w2.shape, full)],
        out_specs=pl.BlockSpec((tm, d), row),
        out_shape=jax.ShapeDtypeStruct((t, d), F32),
        compiler_params=pltpu.CompilerParams(
            dimension_semantics=("parallel",), vmem_limit_bytes=VMEM_LIMIT),
        name="outproj",
    )(x, gate, o_r, o_g, w1, w2)


def _pad_cols(m, width):
    return jnp.pad(m, ((0, 0), (0, width - m.shape[1])))


def kernel(x, c, norm_gain, ada_w, ada_b, ffn_w_gu, ffn_w_down, w_in, w_out, rwkv_mu, rwkv_w0, rwkv_w_up, rwkv_a0, rwkv_a_up, rwkv_g_up, rwkv_k_k, rwkv_k_a, rwkv_r_k, rwkv_gn_w, rwkv_gn_b, vres_w_down, vres_mu, vres_w_up, vres_v0, gdn_conv_w, gdn_a_log, gdn_dt_bias, gdn_norm_w, final_gain):
    bsz, seq, d = x.shape
    depth = norm_gain.shape[0]
    nh = GDN_HEADS
    mod = _adaln(c, ada_w, ada_b)
    xf = x.reshape(bsz * seq, d)
    v_first = None
    row = lambda p: p.reshape(1, -1)
    for l in range(depth):
        def mods(sub):
            m = mod[l * 3 + sub]
            return tuple(m[:, i * d:(i + 1) * d].reshape(bsz, 1, d) for i in range(3))

        shift, scale, gate = mods(0)
        xf = _ffn(xf, norm_gain[l, 0], shift, scale, gate,
                  ffn_w_gu[l, 0].astype(BF16), ffn_w_down[l, 0].astype(BF16), seq)

        shift, scale, gate = mods(1)
        wl = w_in[l]
        w_ba = _pad_cols(wl[:, RWKV_COLS + 4 * GDN_WIDTH:], LANE)
        weights = [wl[:, :RWKV_COLS], wl[:, RWKV_COLS:RWKV_COLS + GDN_QKV],
                   wl[:, RWKV_COLS + GDN_QKV:RWKV_COLS + 4 * GDN_WIDTH], w_ba]
        if l > 0:
            weights.append(_pad_cols(vres_w_down[l - 1], LANE))
        ys = _inproj(xf, norm_gain[l, 1], shift, scale, [wt.astype(BF16) for wt in weights], seq)
        y_r, y_qkv, y_z, y_ba = ys[:4]

        rparams = [row(rwkv_mu[l]), row(rwkv_w0[l]), rwkv_w_up[l], row(rwkv_a0[l]), rwkv_a_up[l],
                   rwkv_g_up[l], row(rwkv_k_k[l]), row(rwkv_k_a[l]), row(rwkv_r_k[l]),
                   row(rwkv_gn_w[l]), row(rwkv_gn_b[l])]
        if l == 0:
            o_r, v_first = _rwkv(y_r, rparams, seq)
        else:
            vparams = [_pad_cols(row(vres_mu[l - 1]), LANE),
                       jnp.pad(vres_w_up[l - 1], ((0, LANE - VRES_RANK), (0, 0))),
                       row(vres_v0[l - 1])]
            (o_r,) = _rwkv(y_r, rparams, seq, y_vres=ys[4], v_first=v_first, vres_params=vparams)

        alog = gdn_a_log[l].astype(F32)
        dtb = gdn_dt_bias[l].astype(F32)
        place = lambda p: jnp.pad(p, (nh, LANE - 2 * nh))
        gparams = [gdn_conv_w[l], row(place(alog)), row(place(dtb)),
                   place(alog)[:SUBLANE].reshape(SUBLANE, 1), place(dtb)[:SUBLANE].reshape(SUBLANE, 1),
                   row(gdn_norm_w[l])]
        o_g = _gdn(y_qkv, y_z, y_ba, gparams, seq)

        wo = w_out[l].astype(BF16)
        xf = _outproj(xf, gate, o_r, o_g, wo[:RWKV_WIDTH], wo[RWKV_WIDTH:], seq)

        shift, scale, gate = mods(2)
        xf = _ffn(xf, norm_gain[l, 2], shift, scale, gate,
                  ffn_w_gu[l, 1].astype(BF16), ffn_w_down[l, 1].astype(BF16), seq,
                  final_gain=final_gain if l == depth - 1 else None)
    return xf.reshape(bsz, seq, d)
```

```python
import functools

import jax
import jax.numpy as jnp
from jax import lax
from jax.experimental import pallas as pl
from jax.experimental.pallas import tpu as pltpu

F32 = jnp.float32
BF16 = jnp.bfloat16
HI = lax.Precision.HIGHEST

RWKV_HEAD_DIM = 64
RWKV_HEADS = 8
RWKV_WIDTH = RWKV_HEAD_DIM * RWKV_HEADS
GDN_HEAD_DIM = 128
GDN_HEADS = 4
GDN_WIDTH = GDN_HEAD_DIM * GDN_HEADS
DECAY_RANK = 64
ICLR_RANK = 64
VRES_RANK = 32
GATE_RANK = 128
RWKV_COLS = 3 * RWKV_WIDTH + DECAY_RANK + ICLR_RANK + GATE_RANK
GDN_QKV = 3 * GDN_WIDTH
CONV_WIDTH = 4
CHUNK = 64
LANE = 128
SUBLANE = 8
MXU_TILE = 256
assert CHUNK == RWKV_HEAD_DIM and MXU_TILE % RWKV_HEAD_DIM == 0 and RWKV_WIDTH % MXU_TILE == 0
assert GDN_HEADS * CHUNK == MXU_TILE
EPS_RMS = 1e-6
EPS_GN = 64e-5
EPS_L2 = 1e-6
MACARON_WEIGHT = 0.5
VMEM_LIMIT = 52 * 1024 * 1024

NT = (((1,), (1,)), ((), ()))
TN = (((0,), (0,)), ((), ()))


def _sigmoid(x):
    return 1.0 / (1.0 + jnp.exp(-x))


def _softplus(x):
    return jnp.maximum(x, 0.0) + jnp.log(1.0 + jnp.exp(-jnp.abs(x)))


def _silu(x):
    return x * _sigmoid(x)


def _dot(a, b, precision=None):
    return jnp.dot(a, b, preferred_element_type=F32, precision=precision)


def _dg(a, b, dims, precision=None):
    return lax.dot_general(a, b, dims, preferred_element_type=F32, precision=precision)


def _bdot(a, b):
    return jnp.dot(a.astype(BF16), b.astype(BF16), preferred_element_type=F32)


def _bdg(a, b, dims):
    return lax.dot_general(a.astype(BF16), b.astype(BF16), dims, preferred_element_type=F32)


def _split2(a):
    hi = a.astype(BF16)
    return hi, (a - hi.astype(F32)).astype(BF16)


def _dot3(a, b):
    ah, al = _split2(a)
    bh, bl = _split2(b)
    d = lambda x, y: jnp.dot(x, y, preferred_element_type=F32)
    return d(ah, bh) + (d(ah, bl) + d(al, bh))


def _split3(a):
    hi = a.astype(BF16)
    r1 = a - hi.astype(F32)
    mid = r1.astype(BF16)
    return hi, mid, (r1 - mid.astype(F32)).astype(BF16)


def _sel_dot(sel, x):
    sb = sel.astype(BF16)
    hi, mid, lo = _split3(x)
    d = lambda y: jnp.dot(sb, y, preferred_element_type=F32)
    return d(hi) + (d(mid) + d(lo))


def _dot_sel(x, sel):
    sb = sel.astype(BF16)
    hi, mid, lo = _split3(x)
    d = lambda y: jnp.dot(y, sb, preferred_element_type=F32)
    return d(hi) + (d(mid) + d(lo))


def _norm_mod(x, gain, shift, scale):
    ms = jnp.mean(x * x, axis=-1, keepdims=True)
    y = x * lax.rsqrt(ms + EPS_RMS) * gain
    return y * (1.0 + scale) + shift


def _tri(n, strict):
    r = lax.broadcasted_iota(jnp.int32, (n, n), 0)
    c = lax.broadcasted_iota(jnp.int32, (n, n), 1)
    return (r > c) if strict else (r >= c)


def _unit_lower_inverse(n, eye, dot):
    t = eye + n
    for _ in range(CHUNK.bit_length() - 2):
        n = dot(n, n)
        t = t + dot(t, n)
    return t


def _mod_kernel(c_ref, w_ref, b_ref, o_ref):
    ca = _silu(c_ref[...])
    o_ref[0] = _dot(ca, w_ref[0], HI) + b_ref[0]


def _adaln(c, ada_w, ada_b):
    n_l, n_sub, d, d3 = ada_w.shape
    b = c.shape[0]
    rows = -(-b // SUBLANE) * SUBLANE
    c_pad = jnp.pad(c, ((0, rows - b), (0, 0)))
    tn = d
    out = pl.pallas_call(
        _mod_kernel,
        grid=(n_l * n_sub, d3 // tn),
        in_specs=[
            pl.BlockSpec((rows, d), lambda i, j: (0, 0)),
            pl.BlockSpec((1, d, tn), lambda i, j: (i, 0, j)),
            pl.BlockSpec((1, 1, tn), lambda i, j: (i, 0, j)),
        ],
        out_specs=pl.BlockSpec((1, rows, tn), lambda i, j: (i, 0, j)),
        out_shape=jax.ShapeDtypeStruct((n_l * n_sub, rows, d3), F32),
        compiler_params=pltpu.CompilerParams(
            dimension_semantics=("parallel", "parallel"), vmem_limit_bytes=VMEM_LIMIT),
        name="adaln_mod",
    )(c_pad, ada_w.reshape(n_l * n_sub, d, d3), ada_b.reshape(n_l * n_sub, 1, d3))
    return out[:, :b, :]


def _ffn_kernel(x_ref, gain_ref, sh_ref, sc_ref, gt_ref, wg_ref, wu_ref, wd_ref, *rest, final):
    if final:
        fg_ref, o_ref, h_scr, acc = rest
    else:
        o_ref, h_scr, acc = rest
    j = pl.program_id(1)

    @pl.when(j == 0)
    def _():
        h = _norm_mod(x_ref[...], gain_ref[...], sh_ref[0], sc_ref[0])
        h_scr[...] = h.astype(BF16)
        acc[...] = jnp.zeros_like(acc)

    h = h_scr[...]
    g = _dot(h, wg_ref[...])
    u = _dot(h, wu_ref[...])
    act = (_silu(g) * u).astype(BF16)
    acc[...] += _dot(act, wd_ref[...])

    @pl.when(j == pl.num_programs(1) - 1)
    def _():
        y = x_ref[...] + (MACARON_WEIGHT * gt_ref[0]) * acc[...]
        if final:
            ms = jnp.mean(y * y, axis=-1, keepdims=True)
            y = y * lax.rsqrt(ms + EPS_RMS) * fg_ref[...]
        o_ref[...] = y


def _ffn(x, gain, shift, scale, gate, w_gu, w_dn, seq, final_gain=None, tm=512, tf=1408):
    t, d = x.shape
    f = w_dn.shape[0]
    tm = min(tm, seq)
    nf = f // tf
    per_b = seq // tm
    final = final_gain is not None
    vec = lambda i, j: (0, 0)
    modspec = pl.BlockSpec((1, 1, d), lambda i, j: (i // per_b, 0, 0))
    in_specs = [
        pl.BlockSpec((tm, d), lambda i, j: (i, 0)),
        pl.BlockSpec((1, d), vec),
        modspec, modspec, modspec,
        pl.BlockSpec((d, tf), lambda i, j: (0, j)),
        pl.BlockSpec((d, tf), lambda i, j: (0, j + nf)),
        pl.BlockSpec((tf, d), lambda i, j: (j, 0)),
    ]
    args = [x, gain.reshape(1, d), shift, scale, gate, w_gu, w_gu, w_dn]
    if final:
        in_specs.append(pl.BlockSpec((1, d), vec))
        args.append(final_gain.reshape(1, d))
    return pl.pallas_call(
        functools.partial(_ffn_kernel, final=final),
        grid=(t // tm, nf),
        in_specs=in_specs,
        out_specs=pl.BlockSpec((tm, d), lambda i, j: (i, 0)),
        out_shape=jax.ShapeDtypeStruct((t, d), F32),
        scratch_shapes=[pltpu.VMEM((tm, d), BF16), pltpu.VMEM((tm, d), F32)],
        compiler_params=pltpu.CompilerParams(
            dimension_semantics=("parallel", "arbitrary"), vmem_limit_bytes=VMEM_LIMIT),
        name="ffn_final" if final else "ffn",
    )(*args)


def _inproj_kernel(x_ref, gain_ref, sh_ref, sc_ref, *rest):
    n = len(rest) // 2
    w_refs, o_refs = rest[:n], rest[n:]
    h = _norm_mod(x_ref[...], gain_ref[...], sh_ref[0], sc_ref[0]).astype(BF16)
    for w_ref, o_ref in zip(w_refs, o_refs):
        o_ref[...] = _dot(h, w_ref[...])


def _inproj(x, gain, shift, scale, weights, seq, tm=512):
    t, d = x.shape
    tm = min(tm, seq)
    per_b = seq // tm
    modspec = pl.BlockSpec((1, 1, d), lambda i: (i // per_b, 0, 0))
    in_specs = [pl.BlockSpec((tm, d), lambda i: (i, 0)), pl.BlockSpec((1, d), lambda i: (0, 0)),
                modspec, modspec]
    in_specs += [pl.BlockSpec(w.shape, lambda i: (0, 0)) for w in weights]
    out_specs = [pl.BlockSpec((tm, w.shape[1]), lambda i: (i, 0)) for w in weights]
    out_shape = [jax.ShapeDtypeStruct((t, w.shape[1]), F32) for w in weights]
    return pl.pallas_call(
        _inproj_kernel,
        grid=(t // tm,),
        in_specs=in_specs,
        out_specs=out_specs,
        out_shape=out_shape,
        compiler_params=pltpu.CompilerParams(
            dimension_semantics=("parallel",), vmem_limit_bytes=VMEM_LIMIT),
        name="inproj",
    )(x, gain.reshape(1, d), shift, scale, *weights)


def _shift_rows(y, carry_row):
    rolled = pltpu.roll(y, 1, 0)
    first = lax.broadcasted_iota(jnp.int32, y.shape, 0) == 0
    return jnp.where(first, carry_row, rolled)


def _rwkv_kernel(*refs, has_vres, tc):
    if has_vres:
        (y_ref, yv_ref, vf_ref, mu_ref, w0_ref, wup_ref, a0_ref, aup_ref, gup_ref, kk_ref, ka_ref,
         rk_ref, gnw_ref, gnb_ref, vmu_ref, vup_ref, v0_ref,
         o_ref, state, carry, carry_v, o_scr) = refs
    else:
        (y_ref, mu_ref, w0_ref, wup_ref, a0_ref, aup_ref, gup_ref, kk_ref, ka_ref,
         rk_ref, gnw_ref, gnb_ref,
         o_ref, v_ref, state, carry, o_scr) = refs
    n, hd, w = RWKV_HEAD_DIM, RWKV_HEADS, RWKV_WIDTH

    @pl.when(pl.program_id(1) == 0)
    def _():
        state[...] = jnp.zeros_like(state)
        carry[...] = jnp.zeros_like(carry)
        if has_vres:
            carry_v[...] = jnp.zeros_like(carry_v)

    y = y_ref[...]
    prev = _shift_rows(y, carry[0:1, :])
    carry[0:1, :] = y[tc - 1:tc, :]
    ym = y + (prev - y) * mu_ref[...]
    r = ym[:, 0:w]
    k = ym[:, w:2 * w]
    v = ym[:, 2 * w:3 * w]
    wd = ym[:, 3 * w:3 * w + DECAY_RANK]
    ad = ym[:, 3 * w + DECAY_RANK:3 * w + DECAY_RANK + ICLR_RANK]
    gd = ym[:, 3 * w + DECAY_RANK + ICLR_RANK:]

    if has_vres:
        yv = yv_ref[...]
        prev_v = _shift_rows(yv, carry_v[0:1, :])
        carry_v[0:1, :] = yv[tc - 1:tc, :]
        yvm = yv + (prev_v - yv) * vmu_ref[...]
        logit = v0_ref[...] + _dot(yvm.astype(BF16), vup_ref[...].astype(BF16))
        v = v + (vf_ref[...] - v) * _sigmoid(logit)
    else:
        v_ref[...] = v

    w_log = -_softplus(-(w0_ref[...] + _dot(jnp.tanh(wd).astype(BF16), wup_ref[...].astype(BF16)))) - 0.5
    logw = -jnp.exp(w_log)
    a = _sigmoid(a0_ref[...] + _dot(ad.astype(BF16), aup_ref[...].astype(BF16)))
    g = _dot(_sigmoid(gd).astype(BF16), gup_ref[...].astype(BF16))

    gw = MXU_TILE
    hpg = gw // n
    groups = [slice(gi * gw, (gi + 1) * gw) for gi in range(w // gw)]
    rr = lax.broadcasted_iota(jnp.int32, (gw, gw), 0)
    cc = lax.broadcasted_iota(jnp.int32, (gw, gw), 1)
    head_lanes = (rr // CHUNK) == (cc // n)
    same_blk = (rr // CHUNK) == (cc // CHUNK)
    eye = (rr == cc).astype(F32)
    bd_strict = jnp.logical_and(same_blk, rr > cc)
    bd_incl = jnp.logical_and(same_blk, rr >= cc)
    l_incl = _tri(CHUNK, False)
    head_ones = (rr // n) == (cc // n)

    def head_sum(x):
        return jnp.concatenate([_dot_sel(x[:, ln], head_ones) for ln in groups], axis=1)

    kkr = k * kk_ref[...]
    kk = kkr * lax.rsqrt(head_sum(kkr * kkr) + EPS_L2)
    k2 = k * (1.0 + (a - 1.0) * ka_ref[...])
    b = kk * a
    av = -kk

    def stack(x, masked):
        s = jnp.concatenate([x] * hpg, axis=0)
        if masked:
            s = jnp.where(head_lanes, s, 0.0)
        return s.astype(BF16)

    for c in range(tc // CHUNK):
        rows = slice(c * CHUNK, (c + 1) * CHUNK)
        lw = logw[rows]
        cum = _sel_dot(l_incl, lw)
        last = cum[CHUNK - 1:CHUNK, :]
        e_pos = jnp.exp(cum)
        e_neg = jnp.exp(-cum)
        e_to_end = jnp.exp(last - cum)
        gam_end = jnp.exp(last)
        at = av[rows] * jnp.exp(cum - lw)
        rt = r[rows] * e_pos
        bt = b[rows] * e_neg
        kt = k2[rows] * e_neg
        bg = b[rows] * e_to_end
        kg = k2[rows] * e_to_end
        vc = v[rows]
        for gi, ln in enumerate(groups):
            x_ar = jnp.concatenate([stack(at[:, ln], True), stack(rt[:, ln], True)], axis=0)
            y_bk = jnp.concatenate([stack(bt[:, ln], False), stack(kt[:, ln], False)], axis=0)
            m1 = _dg(x_ar, y_bk, NT)
            n_ab = jnp.where(bd_strict, m1[:gw, :gw], 0.0)
            a_ak = jnp.where(bd_strict, m1[:gw, gw:], 0.0)
            a_rb = jnp.where(bd_incl, m1[gw:, :gw], 0.0)
            a_rk = jnp.where(bd_incl, m1[gw:, gw:], 0.0)
            s0 = state[gi]
            p = _dg(x_ar, s0.astype(BF16), NT)
            v_bd = stack(vc[:, ln], True)
            t_inv = _unit_lower_inverse(n_ab, eye, _bdot)
            u = _bdot(t_inv, p[:gw] + _dot(a_ak.astype(BF16), v_bd))
            uv = jnp.concatenate([u.astype(BF16), v_bd], axis=0)
            a_r = jnp.concatenate([a_rb, a_rk], axis=1).astype(BF16)
            o_bd = p[gw:] + _dot(a_r, uv)
            o_g = o_bd[0:CHUNK]
            for hh in range(1, hpg):
                o_g = o_g + o_bd[hh * CHUNK:(hh + 1) * CHUNK]
            bk_end = jnp.concatenate([stack(bg[:, ln], True), stack(kg[:, ln], True)], axis=0)
            state[gi] = s0 * gam_end[:, ln] + _dg(uv, bk_end, TN)
            o_scr[rows, ln] = o_g

    o = o_scr[...]
    inv_n = 1.0 / n
    mean = head_sum(o) * inv_n
    dev = o - mean
    var = head_sum(dev * dev) * inv_n
    on = dev * lax.rsqrt(var + EPS_GN) * gnw_ref[...] + gnb_ref[...]
    bonus = head_sum(r * k2 * rk_ref[...])
    o_ref[...] = (on + bonus * v) * g


def _rwkv(y_r, params, seq, y_vres=None, v_first=None, vres_params=None, tc=256):
    t = y_r.shape[0]
    tc = min(tc, seq)
    per_b = seq // tc
    nb = t // seq
    has_vres = y_vres is not None
    w = RWKV_WIDTH
    row = lambda b, s: (b * per_b + s, 0)
    full = lambda b, s: (0, 0)
    in_specs = [pl.BlockSpec((tc, RWKV_COLS), row)]
    args = [y_r]
    if has_vres:
        in_specs += [pl.BlockSpec((tc, LANE), row), pl.BlockSpec((tc, w), row)]
        args += [y_vres, v_first]
    plist = list(params) + (list(vres_params) if has_vres else [])
    in_specs += [pl.BlockSpec(p.shape, full) for p in plist]
    args += plist
    out_specs = [pl.BlockSpec((tc, w), row)]
    out_shape = [jax.ShapeDtypeStruct((t, w), F32)]
    scratch = [pltpu.VMEM((RWKV_WIDTH // MXU_TILE, MXU_TILE, MXU_TILE), F32),
               pltpu.VMEM((SUBLANE, RWKV_COLS), F32)]
    if has_vres:
        scratch.append(pltpu.VMEM((SUBLANE, LANE), F32))
    else:
        out_specs.append(pl.BlockSpec((tc, w), row))
        out_shape.append(jax.ShapeDtypeStruct((t, w), F32))
    scratch.append(pltpu.VMEM((tc, w), F32))
    return pl.pallas_call(
        functools.partial(_rwkv_kernel, has_vres=has_vres, tc=tc),
        grid=(nb, per_b),
        in_specs=in_specs,
        out_specs=out_specs,
        out_shape=out_shape,
        scratch_shapes=scratch,
        compiler_params=pltpu.CompilerParams(
            dimension_semantics=("parallel", "arbitrary"), vmem_limit_bytes=VMEM_LIMIT),
        name="rwkv7_vres" if has_vres else "rwkv7",
    )(*args)


def _gdn_kernel(qkv_ref, z_ref, ba_ref, cw_ref, alog_r_ref, dt_r_ref, alog_c_ref, dt_c_ref, nw_ref,
                o_ref, state, ext, o_scr, *, tc):
    dh, nh, w = GDN_HEAD_DIM, GDN_HEADS, GDN_WIDTH
    pad = SUBLANE

    @pl.when(pl.program_id(1) == 0)
    def _():
        state[...] = jnp.zeros_like(state)
        ext[0:pad, :] = jnp.zeros((pad, GDN_QKV), F32)

    x = qkv_ref[...]
    ext[pad:pad + tc, :] = x
    conv = x * cw_ref[CONV_WIDTH - 1:CONV_WIDTH, :]
    for j in range(CONV_WIDTH - 1):
        back = CONV_WIDTH - 1 - j
        conv = conv + ext[pad - back:pad - back + tc, :] * cw_ref[j:j + 1, :]
    ext[0:pad, :] = x[tc - pad:tc, :]
    qkv = _silu(conv)

    ba = ba_ref[...]
    beta_c = _sigmoid(ba)
    g_c = -jnp.exp(alog_r_ref[...]) * _softplus(ba + dt_r_ref[...])
    er = lax.broadcasted_iota(jnp.int32, (pad, LANE), 0)
    ec = lax.broadcasted_iota(jnp.int32, (pad, LANE), 1)
    pick = (er == ec).astype(BF16)
    ba_hi, ba_mid, ba_lo = _split3(ba)
    ba_t = _dg(pick, ba_hi, NT) + (_dg(pick, ba_mid, NT) + _dg(pick, ba_lo, NT))
    g_r = -jnp.exp(alog_c_ref[...]) * _softplus(ba_t + dt_c_ref[...])

    gw = nh * CHUNK
    rr = lax.broadcasted_iota(jnp.int32, (gw, gw), 0)
    cc = lax.broadcasted_iota(jnp.int32, (gw, gw), 1)
    same_blk = (rr // CHUNK) == (cc // CHUNK)
    eye = (rr == cc).astype(F32)
    bd_strict = jnp.logical_and(same_blk, rr > cc)
    bd_incl = jnp.logical_and(same_blk, rr >= cc)
    l_bd = bd_incl
    u_bd = jnp.logical_and(same_blk, rr <= cc)

    def stack_heads(x):
        return jnp.concatenate([x[:, h * dh:(h + 1) * dh] for h in range(nh)], axis=0)

    for c in range(tc // CHUNK):
        rows = slice(c * CHUNK, (c + 1) * CHUNK)
        g4 = jnp.concatenate([g_c[rows, nh + h:nh + h + 1] for h in range(nh)], axis=0)
        beta = jnp.concatenate([beta_c[rows, h:h + 1] for h in range(nh)], axis=0)
        ccol = _sel_dot(l_bd, jnp.broadcast_to(g4, (gw, LANE)))[:, 0:1]
        g_row = jnp.concatenate([g_r[nh + h:nh + h + 1, rows] for h in range(nh)], axis=1)
        crow = _dot_sel(jnp.broadcast_to(g_row, (SUBLANE, gw)), u_bd)[0:1, :]
        clast = jnp.concatenate(
            [jnp.broadcast_to(ccol[(h + 1) * CHUNK - 1:(h + 1) * CHUNK, :], (CHUNK, 1)) for h in range(nh)],
            axis=0)
        decay = jnp.where(bd_incl, jnp.exp(jnp.where(bd_incl, ccol - crow, 0.0)), 0.0)
        qs = stack_heads(qkv[rows, 0:w])
        ks = stack_heads(qkv[rows, w:2 * w])
        vs = stack_heads(qkv[rows, 2 * w:3 * w])
        qs = qs * lax.rsqrt(jnp.sum(qs * qs, axis=-1, keepdims=True) + EPS_L2) * (dh ** -0.5)
        ks = ks * lax.rsqrt(jnp.sum(ks * ks, axis=-1, keepdims=True) + EPS_L2)
        ksb = ks.astype(BF16)
        kkm = _dg(ksb, ksb, NT)
        neg_n = -jnp.where(bd_strict, kkm * decay * beta, 0.0)
        e_c = jnp.exp(ccol)
        rhs = jnp.concatenate([vs * beta, ks * (beta * e_c)], axis=1)
        sol = _dot3(_unit_lower_inverse(neg_n, eye, _dot3), rhs)
        u = sol[:, :dh]
        wmat = sol[:, dh:]
        p = _dg(qs.astype(BF16), ksb, NT) * decay
        q_dec = qs * e_c
        k_dec = ks * jnp.exp(clast - ccol)
        v_new, o_st = [], []
        for h in range(nh):
            hr = slice(h * CHUNK, (h + 1) * CHUNK)
            st = state[h].astype(BF16)
            wq = jnp.concatenate([wmat[hr], q_dec[hr]], axis=0).astype(BF16)
            both = _dot(wq, st)
            v_new.append(u[hr] - both[:CHUNK])
            o_st.append(both[CHUNK:])
        v_new = jnp.concatenate(v_new, axis=0)
        o_all = jnp.concatenate(o_st, axis=0) + _bdot(p, v_new)
        for h in range(nh):
            hr = slice(h * CHUNK, (h + 1) * CHUNK)
            state[h] = (state[h] * jnp.exp(clast[h * CHUNK:h * CHUNK + 1, :])
                        + _bdg(k_dec[hr], v_new[hr], TN))
            o_scr[rows, h * dh:(h + 1) * dh] = o_all[hr]

    zs = _silu(z_ref[...])
    for h in range(nh):
        ln = slice(h * dh, (h + 1) * dh)
        o = o_scr[:, ln]
        o = o * lax.rsqrt(jnp.mean(o * o, axis=-1, keepdims=True) + EPS_RMS) * nw_ref[...]
        o_ref[:, ln] = o * zs[:, ln]


def _gdn(y_qkv, y_z, y_ba, params, seq, tc=256):
    t = y_qkv.shape[0]
    tc = min(tc, seq)
    per_b = seq // tc
    nb = t // seq
    w = GDN_WIDTH
    row = lambda b, s: (b * per_b + s, 0)
    full = lambda b, s: (0, 0)
    in_specs = [pl.BlockSpec((tc, GDN_QKV), row), pl.BlockSpec((tc, w), row), pl.BlockSpec((tc, LANE), row)]
    in_specs += [pl.BlockSpec(p.shape, full) for p in params]
    return pl.pallas_call(
        functools.partial(_gdn_kernel, tc=tc),
        grid=(nb, per_b),
        in_specs=in_specs,
        out_specs=pl.BlockSpec((tc, w), row),
        out_shape=jax.ShapeDtypeStruct((t, w), F32),
        scratch_shapes=[pltpu.VMEM((GDN_HEADS, GDN_HEAD_DIM, GDN_HEAD_DIM), F32),
                        pltpu.VMEM((tc + SUBLANE, GDN_QKV), F32),
                        pltpu.VMEM((tc, w), F32)],
        compiler_params=pltpu.CompilerParams(
            dimension_semantics=("parallel", "arbitrary"), vmem_limit_bytes=VMEM_LIMIT),
        name="gdn",
    )(y_qkv, y_z, y_ba, *params)


def _outproj_kernel(x_ref, gt_ref, or_ref, og_ref, w1_ref, w2_ref, o_ref):
    mixed = _dot(or_ref[...].astype(BF16), w1_ref[...]) + _dot(og_ref[...].astype(BF16), w2_ref[...])
    o_ref[...] = x_ref[...] + gt_ref[0] * mixed


def _outproj(x, gate, o_r, o_g, w1, w2, seq, tm=512):
    t, d = x.shape
    tm = min(tm, seq)
    per_b = seq // tm
    row = lambda i: (i, 0)
    full = lambda i: (0, 0)
    return pl.pallas_call(
        _outproj_kernel,
        grid=(t // tm,),
        in_specs=[pl.BlockSpec((tm, d), row),
                  pl.BlockSpec((1, 1, d), lambda i: (i // per_b, 0, 0)),
                  pl.BlockSpec((tm, o_r.shape[1]), row), pl.BlockSpec((tm, o_g.shape[1]), row),
                  pl.BlockSpec(w1.shape, full), pl.BlockSpec(w2.shape, full)],
        out_specs=pl.BlockSpec((tm, d), row),
        out_shape=jax.ShapeDtypeStruct((t, d), F32),
        compiler_params=pltpu.CompilerParams(
            dimension_semantics=("parallel",), vmem_limit_bytes=VMEM_LIMIT),
        name="outproj",
    )(x, gate, o_r, o_g, w1, w2)


def _pad_cols(m, width):
    return jnp.pad(m, ((0, 0), (0, width - m.shape[1])))


def kernel(x, c, norm_gain, ada_w, ada_b, ffn_w_gu, ffn_w_down, w_in, w_out, rwkv_mu, rwkv_w0, rwkv_w_up, rwkv_a0, rwkv_a_up, rwkv_g_up, rwkv_k_k, rwkv_k_a, rwkv_r_k, rwkv_gn_w, rwkv_gn_b, vres_w_down, vres_mu, vres_w_up, vres_v0, gdn_conv_w, gdn_a_log, gdn_dt_bias, gdn_norm_w, final_gain):
    bsz, seq, d = x.shape
    depth = norm_gain.shape[0]
    nh = GDN_HEADS
    mod = _adaln(c, ada_w, ada_b)
    xf = x.reshape(bsz * seq, d)
    v_first = None
    row = lambda p: p.reshape(1, -1)
    for l in range(depth):
        def mods(sub):
            m = mod[l * 3 + sub]
            return tuple(m[:, i * d:(i + 1) * d].reshape(bsz, 1, d) for i in range(3))

        shift, scale, gate = mods(0)
        xf = _ffn(xf, norm_gain[l, 0], shift, scale, gate,
                  ffn_w_gu[l, 0].astype(BF16), ffn_w_down[l, 0].astype(BF16), seq)

        shift, scale, gate = mods(1)
        wl = w_in[l]
        w_ba = _pad_cols(wl[:, RWKV_COLS + 4 * GDN_WIDTH:], LANE)
        weights = [wl[:, :RWKV_COLS], wl[:, RWKV_COLS:RWKV_COLS + GDN_QKV],
                   wl[:, RWKV_COLS + GDN_QKV:RWKV_COLS + 4 * GDN_WIDTH], w_ba]
        if l > 0:
            weights.append(_pad_cols(vres_w_down[l - 1], LANE))
        ys = _inproj(xf, norm_gain[l, 1], shift, scale, [wt.astype(BF16) for wt in weights], seq)
        y_r, y_qkv, y_z, y_ba = ys[:4]

        rparams = [row(rwkv_mu[l]), row(rwkv_w0[l]), rwkv_w_up[l], row(rwkv_a0[l]), rwkv_a_up[l],
                   rwkv_g_up[l], row(rwkv_k_k[l]), row(rwkv_k_a[l]), row(rwkv_r_k[l]),
                   row(rwkv_gn_w[l]), row(rwkv_gn_b[l])]
        if l == 0:
            o_r, v_first = _rwkv(y_r, rparams, seq)
        else:
            vparams = [_pad_cols(row(vres_mu[l - 1]), LANE),
                       jnp.pad(vres_w_up[l - 1], ((0, LANE - VRES_RANK), (0, 0))),
                       row(vres_v0[l - 1])]
            (o_r,) = _rwkv(y_r, rparams, seq, y_vres=ys[4], v_first=v_first, vres_params=vparams)

        alog = gdn_a_log[l].astype(F32)
        dtb = gdn_dt_bias[l].astype(F32)
        place = lambda p: jnp.pad(p, (nh, LANE - 2 * nh))
        gparams = [gdn_conv_w[l], row(place(alog)), row(place(dtb)),
                   place(alog)[:SUBLANE].reshape(SUBLANE, 1), place(dtb)[:SUBLANE].reshape(SUBLANE, 1),
                   row(gdn_norm_w[l])]
        o_g = _gdn(y_qkv, y_z, y_ba, gparams, seq)

        wo = w_out[l].astype(BF16)
        xf = _outproj(xf, gate, o_r, o_g, wo[:RWKV_WIDTH], wo[RWKV_WIDTH:], seq)

        shift, scale, gate = mods(2)
        xf = _ffn(xf, norm_gain[l, 2], shift, scale, gate,
                  ffn_w_gu[l, 1].astype(BF16), ffn_w_down[l, 1].astype(BF16), seq,
                  final_gain=final_gain if l == depth - 1 else None)
    return xf.reshape(bsz, seq, d)
```

```python
import functools

import jax
import jax.numpy as jnp
from jax import lax
from jax.experimental import pallas as pl
from jax.experimental.pallas import tpu as pltpu

F32 = jnp.float32
BF16 = jnp.bfloat16
HI = lax.Precision.HIGHEST

RWKV_HEAD_DIM = 64
RWKV_HEADS = 8
RWKV_WIDTH = RWKV_HEAD_DIM * RWKV_HEADS
GDN_HEAD_DIM = 128
GDN_HEADS = 4
GDN_WIDTH = GDN_HEAD_DIM * GDN_HEADS
DECAY_RANK = 64
ICLR_RANK = 64
VRES_RANK = 32
GATE_RANK = 128
RWKV_COLS = 3 * RWKV_WIDTH + DECAY_RANK + ICLR_RANK + GATE_RANK
GDN_QKV = 3 * GDN_WIDTH
CONV_WIDTH = 4
CHUNK = 64
LANE = 128
SUBLANE = 8
MXU_TILE = 256
assert CHUNK == RWKV_HEAD_DIM and MXU_TILE % RWKV_HEAD_DIM == 0 and RWKV_WIDTH % MXU_TILE == 0
assert GDN_HEADS * CHUNK == MXU_TILE
EPS_RMS = 1e-6
EPS_GN = 64e-5
EPS_L2 = 1e-6
MACARON_WEIGHT = 0.5
VMEM_LIMIT = 52 * 1024 * 1024

NT = (((1,), (1,)), ((), ()))
TN = (((0,), (0,)), ((), ()))


def _sigmoid(x):
    return 1.0 / (1.0 + jnp.exp(-x))


def _softplus(x):
    return jnp.maximum(x, 0.0) + jnp.log(1.0 + jnp.exp(-jnp.abs(x)))


def _silu(x):
    return x * _sigmoid(x)


def _dot(a, b, precision=None):
    return jnp.dot(a, b, preferred_element_type=F32, precision=precision)


def _dg(a, b, dims, precision=None):
    return lax.dot_general(a, b, dims, preferred_element_type=F32, precision=precision)


def _bdot(a, b):
    return jnp.dot(a.astype(BF16), b.astype(BF16), preferred_element_type=F32)


def _bdg(a, b, dims):
    return lax.dot_general(a.astype(BF16), b.astype(BF16), dims, preferred_element_type=F32)


def _split2(a):
    hi = a.astype(BF16)
    return hi, (a - hi.astype(F32)).astype(BF16)


def _dot3(a, b):
    ah, al = _split2(a)
    bh, bl = _split2(b)
    d = lambda x, y: jnp.dot(x, y, preferred_element_type=F32)
    return d(ah, bh) + (d(ah, bl) + d(al, bh))


def _split3(a):
    hi = a.astype(BF16)
    r1 = a - hi.astype(F32)
    mid = r1.astype(BF16)
    return hi, mid, (r1 - mid.astype(F32)).astype(BF16)


def _sel_dot(sel, x):
    sb = sel.astype(BF16)
    hi, mid, lo = _split3(x)
    d = lambda y: jnp.dot(sb, y, preferred_element_type=F32)
    return d(hi) + (d(mid) + d(lo))


def _dot_sel(x, sel):
    sb = sel.astype(BF16)
    hi, mid, lo = _split3(x)
    d = lambda y: jnp.dot(y, sb, preferred_element_type=F32)
    return d(hi) + (d(mid) + d(lo))


def _norm_mod(x, gain, shift, scale):
    ms = jnp.mean(x * x, axis=-1, keepdims=True)
    y = x * lax.rsqrt(ms + EPS_RMS) * gain
    return y * (1.0 + scale) + shift


def _tri(n, strict):
    r = lax.broadcasted_iota(jnp.int32, (n, n), 0)
    c = lax.broadcasted_iota(jnp.int32, (n, n), 1)
    return (r > c) if strict else (r >= c)


def _mod_kernel(c_ref, w_ref, b_ref, o_ref):
    ca = _silu(c_ref[...])
    o_ref[0] = _dot(ca, w_ref[0], HI) + b_ref[0]


def _adaln(c, ada_w, ada_b):
    n_l, n_sub, d, d3 = ada_w.shape
    b = c.shape[0]
    rows = -(-b // SUBLANE) * SUBLANE
    c_pad = jnp.pad(c, ((0, rows - b), (0, 0)))
    tn = d
    out = pl.pallas_call(
        _mod_kernel,
        grid=(n_l * n_sub, d3 // tn),
        in_specs=[
            pl.BlockSpec((rows, d), lambda i, j: (0, 0)),
            pl.BlockSpec((1, d, tn), lambda i, j: (i, 0, j)),
            pl.BlockSpec((1, 1, tn), lambda i, j: (i, 0, j)),
        ],
        out_specs=pl.BlockSpec((1, rows, tn), lambda i, j: (i, 0, j)),
        out_shape=jax.ShapeDtypeStruct((n_l * n_sub, rows, d3), F32),
        compiler_params=pltpu.CompilerParams(
            dimension_semantics=("parallel", "parallel"), vmem_limit_bytes=VMEM_LIMIT),
        name="adaln_mod",
    )(c_pad, ada_w.reshape(n_l * n_sub, d, d3), ada_b.reshape(n_l * n_sub, 1, d3))
    return out[:, :b, :]


def _ffn_kernel(x_ref, gain_ref, sh_ref, sc_ref, gt_ref, wg_ref, wu_ref, wd_ref, *rest, final):
    if final:
        fg_ref, o_ref, h_scr, acc = rest
    else:
        o_ref, h_scr, acc = rest
    j = pl.program_id(1)

    @pl.when(j == 0)
    def _():
        h = _norm_mod(x_ref[...], gain_ref[...], sh_ref[0], sc_ref[0])
        h_scr[...] = h.astype(BF16)
        acc[...] = jnp.zeros_like(acc)

    h = h_scr[...]
    g = _dot(h, wg_ref[...])
    u = _dot(h, wu_ref[...])
    act = (_silu(g) * u).astype(BF16)
    acc[...] += _dot(act, wd_ref[...])

    @pl.when(j == pl.num_programs(1) - 1)
    def _():
        y = x_ref[...] + (MACARON_WEIGHT * gt_ref[0]) * acc[...]
        if final:
            ms = jnp.mean(y * y, axis=-1, keepdims=True)
            y = y * lax.rsqrt(ms + EPS_RMS) * fg_ref[...]
        o_ref[...] = y


def _ffn(x, gain, shift, scale, gate, w_gu, w_dn, seq, final_gain=None, tm=512, tf=1408):
    t, d = x.shape
    f = w_dn.shape[0]
    tm = min(tm, seq)
    nf = f // tf
    per_b = seq // tm
    final = final_gain is not None
    vec = lambda i, j: (0, 0)
    modspec = pl.BlockSpec((1, 1, d), lambda i, j: (i // per_b, 0, 0))
    in_specs = [
        pl.BlockSpec((tm, d), lambda i, j: (i, 0)),
        pl.BlockSpec((1, d), vec),
        modspec, modspec, modspec,
        pl.BlockSpec((d, tf), lambda i, j: (0, j)),
        pl.BlockSpec((d, tf), lambda i, j: (0, j + nf)),
        pl.BlockSpec((tf, d), lambda i, j: (j, 0)),
    ]
    args = [x, gain.reshape(1, d), shift, scale, gate, w_gu, w_gu, w_dn]
    if final:
        in_specs.append(pl.BlockSpec((1, d), vec))
        args.append(final_gain.reshape(1, d))
    return pl.pallas_call(
        functools.partial(_ffn_kernel, final=final),
        grid=(t // tm, nf),
        in_specs=in_specs,
        out_specs=pl.BlockSpec((tm, d), lambda i, j: (i, 0)),
        out_shape=jax.ShapeDtypeStruct((t, d), F32),
        scratch_shapes=[pltpu.VMEM((tm, d), BF16), pltpu.VMEM((tm, d), F32)],
        compiler_params=pltpu.CompilerParams(
            dimension_semantics=("parallel", "arbitrary"), vmem_limit_bytes=VMEM_LIMIT),
        name="ffn_final" if final else "ffn",
    )(*args)


def _inproj_kernel(x_ref, gain_ref, sh_ref, sc_ref, *rest):
    n = len(rest) // 2
    w_refs, o_refs = rest[:n], rest[n:]
    h = _norm_mod(x_ref[...], gain_ref[...], sh_ref[0], sc_ref[0]).astype(BF16)
    for w_ref, o_ref in zip(w_refs, o_refs):
        o_ref[...] = _dot(h, w_ref[...])


def _inproj(x, gain, shift, scale, weights, seq, tm=512):
    t, d = x.shape
    tm = min(tm, seq)
    per_b = seq // tm
    modspec = pl.BlockSpec((1, 1, d), lambda i: (i // per_b, 0, 0))
    in_specs = [pl.BlockSpec((tm, d), lambda i: (i, 0)), pl.BlockSpec((1, d), lambda i: (0, 0)),
                modspec, modspec]
    in_specs += [pl.BlockSpec(w.shape, lambda i: (0, 0)) for w in weights]
    out_specs = [pl.BlockSpec((tm, w.shape[1]), lambda i: (i, 0)) for w in weights]
    out_shape = [jax.ShapeDtypeStruct((t, w.shape[1]), F32) for w in weights]
    return pl.pallas_call(
        _inproj_kernel,
        grid=(t // tm,),
        in_specs=in_specs,
        out_specs=out_specs,
        out_shape=out_shape,
        compiler_params=pltpu.CompilerParams(
            dimension_semantics=("parallel",), vmem_limit_bytes=VMEM_LIMIT),
        name="inproj",
    )(x, gain.reshape(1, d), shift, scale, *weights)


def _shift_rows(y, carry_row):
    rolled = pltpu.roll(y, 1, 0)
    first = lax.broadcasted_iota(jnp.int32, y.shape, 0) == 0
    return jnp.where(first, carry_row, rolled)


def _rwkv_kernel(*refs, has_vres, tc):
    if has_vres:
        (y_ref, yv_ref, vf_ref, mu_ref, w0_ref, wup_ref, a0_ref, aup_ref, gup_ref, kk_ref, ka_ref,
         rk_ref, gnw_ref, gnb_ref, vmu_ref, vup_ref, v0_ref,
         o_ref, state, carry, carry_v, o_scr) = refs
    else:
        (y_ref, mu_ref, w0_ref, wup_ref, a0_ref, aup_ref, gup_ref, kk_ref, ka_ref,
         rk_ref, gnw_ref, gnb_ref,
         o_ref, v_ref, state, carry, o_scr) = refs
    n, hd, w = RWKV_HEAD_DIM, RWKV_HEADS, RWKV_WIDTH

    @pl.when(pl.program_id(1) == 0)
    def _():
        state[...] = jnp.zeros_like(state)
        carry[...] = jnp.zeros_like(carry)
        if has_vres:
            carry_v[...] = jnp.zeros_like(carry_v)

    y = y_ref[...]
    prev = _shift_rows(y, carry[0:1, :])
    carry[0:1, :] = y[tc - 1:tc, :]
    ym = y + (prev - y) * mu_ref[...]
    r = ym[:, 0:w]
    k = ym[:, w:2 * w]
    v = ym[:, 2 * w:3 * w]
    wd = ym[:, 3 * w:3 * w + DECAY_RANK]
    ad = ym[:, 3 * w + DECAY_RANK:3 * w + DECAY_RANK + ICLR_RANK]
    gd = ym[:, 3 * w + DECAY_RANK + ICLR_RANK:]

    if has_vres:
        yv = yv_ref[...]
        prev_v = _shift_rows(yv, carry_v[0:1, :])
        carry_v[0:1, :] = yv[tc - 1:tc, :]
        yvm = yv + (prev_v - yv) * vmu_ref[...]
        logit = v0_ref[...] + _dot(yvm.astype(BF16), vup_ref[...].astype(BF16))
        v = v + (vf_ref[...] - v) * _sigmoid(logit)
    else:
        v_ref[...] = v

    w_log = -_softplus(-(w0_ref[...] + _dot(jnp.tanh(wd).astype(BF16), wup_ref[...].astype(BF16)))) - 0.5
    logw = -jnp.exp(w_log)
    a = _sigmoid(a0_ref[...] + _dot(ad.astype(BF16), aup_ref[...].astype(BF16)))
    g = _dot(_sigmoid(gd).astype(BF16), gup_ref[...].astype(BF16))

    gw = MXU_TILE
    hpg = gw // n
    groups = [slice(gi * gw, (gi + 1) * gw) for gi in range(w // gw)]
    rr = lax.broadcasted_iota(jnp.int32, (gw, gw), 0)
    cc = lax.broadcasted_iota(jnp.int32, (gw, gw), 1)
    same_blk = (rr // CHUNK) == (cc // CHUNK)
    head_ones = (rr // n) == (cc // n)
    tt = lax.broadcasted_iota(jnp.int32, (CHUNK, gw), 0)
    ss = lax.broadcasted_iota(jnp.int32, (CHUNK, gw), 1) % CHUNK
    ls_strict = tt > ss
    ls_incl = tt >= ss
    eye_ls = (tt == ss).astype(F32)
    l_incl = _tri(CHUNK, False)

    def head_sum(x):
        return jnp.concatenate([_dot_sel(x[:, ln], head_ones) for ln in groups], axis=1)

    def bd(x):
        return jnp.where(same_blk, jnp.concatenate([x] * hpg, axis=0), 0.0)

    def ls_dot(a, b_bd):
        return jnp.dot(a.astype(BF16), b_bd.astype(BF16), preferred_element_type=F32)

    kkr = k * kk_ref[...]
    kk = kkr * lax.rsqrt(head_sum(kkr * kkr) + EPS_L2)
    k2 = k * (1.0 + (a - 1.0) * ka_ref[...])
    b = kk * a
    av = -kk

    n_chunks = tc // CHUNK
    insts = [(c, gi) for c in range(n_chunks) for gi in range(len(groups))]
    d = {}
    for c in range(n_chunks):
        rows = slice(c * CHUNK, (c + 1) * CHUNK)
        lw = logw[rows]
        cum = _sel_dot(l_incl, lw)
        last = cum[CHUNK - 1:CHUNK, :]
        e_pos = jnp.exp(cum)
        e_neg = jnp.exp(-cum)
        e_to_end = jnp.exp(last - cum)
        gam_end = jnp.exp(last)
        at = av[rows] * jnp.exp(cum - lw)
        rt = r[rows] * e_pos
        bt = b[rows] * e_neg
        kt = k2[rows] * e_neg
        bg = b[rows] * e_to_end
        kg = k2[rows] * e_to_end
        vc = v[rows]
        for gi, ln in enumerate(groups):
            d[c, gi] = dict(at=at[:, ln], rt=rt[:, ln], bt=bt[:, ln], kt=kt[:, ln], bg=bg[:, ln],
                            kg=kg[:, ln], v=vc[:, ln], gam=gam_end[:, ln])
    for i in insts:
        e = d[i]
        x_ar = jnp.concatenate([e['at'], e['rt']], axis=0).astype(BF16)
        y_bk = jnp.concatenate([bd(e['bt']), bd(e['kt'])], axis=0).astype(BF16)
        m1 = _dg(x_ar, y_bk, NT)
        e['n'] = jnp.where(ls_strict, m1[:CHUNK, :gw], 0.0)
        e['a_ak'] = jnp.where(ls_strict, m1[:CHUNK, gw:], 0.0)
        e['a_rb'] = jnp.where(ls_incl, m1[CHUNK:, :gw], 0.0)
        e['a_rk'] = jnp.where(ls_incl, m1[CHUNK:, gw:], 0.0)
        e['t'] = eye_ls + e['n']
        e['v_bd'] = bd(e['v']).astype(BF16)
    for _ in range(CHUNK.bit_length() - 2):
        for i in insts:
            d[i]['n'] = ls_dot(d[i]['n'], bd(d[i]['n']))
        for i in insts:
            d[i]['t'] = d[i]['t'] + ls_dot(d[i]['t'], bd(d[i]['n']))
    for i in insts:
        e = d[i]
        e['w'] = ls_dot(e['t'], bd(e['at']))
        e['akv'] = ls_dot(e['a_ak'], e['v_bd'])
    for i in insts:
        e = d[i]
        e['u0'] = ls_dot(e['t'], bd(e['akv']))
        e['q'] = e['rt'] + ls_dot(e['a_rb'], bd(e['w']))
    for i in insts:
        e = d[i]
        a_r = jnp.concatenate([e['a_rb'], e['a_rk']], axis=1)
        e['o0'] = ls_dot(a_r, jnp.concatenate([bd(e['u0']).astype(BF16), e['v_bd']], axis=0))
        e['m'] = jnp.where(same_blk, _bdg(e['w'], e['bg'], TN), 0.0)
        uv = jnp.concatenate([e['u0'], e['v']], axis=0)
        bk_end = jnp.concatenate([e['bg'], e['kg']], axis=0)
        e['n0'] = jnp.where(same_blk, _bdg(uv, bk_end, TN), 0.0)
    for c, gi in insts:
        e = d[c, gi]
        s0 = state[gi]
        sb = s0.astype(BF16)
        o_scr[c * CHUNK:(c + 1) * CHUNK, groups[gi]] = _dg(e['q'].astype(BF16), sb, NT) + e['o0']
        state[gi] = s0 * e['gam'] + jnp.dot(sb, e['m'].astype(BF16), preferred_element_type=F32) + e['n0']

    o = o_scr[...]
    inv_n = 1.0 / n
    mean = head_sum(o) * inv_n
    dev = o - mean
    var = head_sum(dev * dev) * inv_n
    on = dev * lax.rsqrt(var + EPS_GN) * gnw_ref[...] + gnb_ref[...]
    bonus = head_sum(r * k2 * rk_ref[...])
    o_ref[...] = (on + bonus * v) * g


def _rwkv(y_r, params, seq, y_vres=None, v_first=None, vres_params=None, tc=256):
    t = y_r.shape[0]
    tc = min(tc, seq)
    per_b = seq // tc
    nb = t // seq
    has_vres = y_vres is not None
    w = RWKV_WIDTH
    row = lambda b, s: (b * per_b + s, 0)
    full = lambda b, s: (0, 0)
    in_specs = [pl.BlockSpec((tc, RWKV_COLS), row)]
    args = [y_r]
    if has_vres:
        in_specs += [pl.BlockSpec((tc, LANE), row), pl.BlockSpec((tc, w), row)]
        args += [y_vres, v_first]
    plist = list(params) + (list(vres_params) if has_vres else [])
    in_specs += [pl.BlockSpec(p.shape, full) for p in plist]
    args += plist
    out_specs = [pl.BlockSpec((tc, w), row)]
    out_shape = [jax.ShapeDtypeStruct((t, w), F32)]
    scratch = [pltpu.VMEM((RWKV_WIDTH // MXU_TILE, MXU_TILE, MXU_TILE), F32),
               pltpu.VMEM((SUBLANE, RWKV_COLS), F32)]
    if has_vres:
        scratch.append(pltpu.VMEM((SUBLANE, LANE), F32))
    else:
        out_specs.append(pl.BlockSpec((tc, w), row))
        out_shape.append(jax.ShapeDtypeStruct((t, w), F32))
    scratch.append(pltpu.VMEM((tc, w), F32))
    return pl.pallas_call(
        functools.partial(_rwkv_kernel, has_vres=has_vres, tc=tc),
        grid=(nb, per_b),
        in_specs=in_specs,
        out_specs=out_specs,
        out_shape=out_shape,
        scratch_shapes=scratch,
        compiler_params=pltpu.CompilerParams(
            dimension_semantics=("parallel", "arbitrary"), vmem_limit_bytes=VMEM_LIMIT),
        name="rwkv7_vres" if has_vres else "rwkv7",
    )(*args)


def _gdn_kernel(qkv_ref, z_ref, ba_ref, cw_ref, alog_r_ref, dt_r_ref, alog_c_ref, dt_c_ref, nw_ref,
                o_ref, state, ext, o_scr, *, tc):
    dh, nh, w = GDN_HEAD_DIM, GDN_HEADS, GDN_WIDTH
    pad = SUBLANE

    @pl.when(pl.program_id(1) == 0)
    def _():
        state[...] = jnp.zeros_like(state)
        ext[0:pad, :] = jnp.zeros((pad, GDN_QKV), F32)

    x = qkv_ref[...]
    ext[pad:pad + tc, :] = x
    conv = x * cw_ref[CONV_WIDTH - 1:CONV_WIDTH, :]
    for j in range(CONV_WIDTH - 1):
        back = CONV_WIDTH - 1 - j
        conv = conv + ext[pad - back:pad - back + tc, :] * cw_ref[j:j + 1, :]
    ext[0:pad, :] = x[tc - pad:tc, :]
    qkv = _silu(conv)

    ba = ba_ref[...]
    beta_c = _sigmoid(ba)
    g_c = -jnp.exp(alog_r_ref[...]) * _softplus(ba + dt_r_ref[...])
    er = lax.broadcasted_iota(jnp.int32, (pad, LANE), 0)
    ec = lax.broadcasted_iota(jnp.int32, (pad, LANE), 1)
    pick = (er == ec).astype(BF16)
    ba_hi, ba_mid, ba_lo = _split3(ba)
    ba_t = _dg(pick, ba_hi, NT) + (_dg(pick, ba_mid, NT) + _dg(pick, ba_lo, NT))
    g_r = -jnp.exp(alog_c_ref[...]) * _softplus(ba_t + dt_c_ref[...])

    gw = nh * CHUNK
    rr = lax.broadcasted_iota(jnp.int32, (gw, gw), 0)
    cc = lax.broadcasted_iota(jnp.int32, (gw, gw), 1)
    same_blk = (rr // CHUNK) == (cc // CHUNK)
    u_bd = jnp.logical_and(same_blk, rr <= cc)
    tt = lax.broadcasted_iota(jnp.int32, (CHUNK, gw), 0)
    ss = lax.broadcasted_iota(jnp.int32, (CHUNK, gw), 1) % CHUNK
    ls_strict = tt > ss
    ls_incl = tt >= ss
    eye_ls = (tt == ss).astype(F32)
    l_incl = _tri(CHUNK, False)
    src = lax.broadcasted_iota(jnp.int32, (LANE, gw), 0)
    spread_g = src == nh + lax.broadcasted_iota(jnp.int32, (LANE, gw), 1) // CHUNK
    spread_b = src == lax.broadcasted_iota(jnp.int32, (LANE, gw), 1) // CHUNK
    srcw = lax.broadcasted_iota(jnp.int32, (LANE, w), 0)
    spread_gw = srcw == nh + lax.broadcasted_iota(jnp.int32, (LANE, w), 1) // dh
    spread_bw = srcw == lax.broadcasted_iota(jnp.int32, (LANE, w), 1) // dh
    head_rows = (lax.broadcasted_iota(jnp.int32, (gw, w), 0) // CHUNK
                 == lax.broadcasted_iota(jnp.int32, (gw, w), 1) // dh)

    def bd(x):
        return jnp.where(same_blk, jnp.concatenate([x] * nh, axis=0), 0.0)

    def bd_wide(x):
        return jnp.where(head_rows, jnp.concatenate([x] * nh, axis=0), 0.0)

    def l2norm_heads(x):
        parts = []
        for h in range(nh):
            xh = x[:, h * dh:(h + 1) * dh]
            parts.append(xh * lax.rsqrt(jnp.sum(xh * xh, axis=-1, keepdims=True) + EPS_L2))
        return jnp.concatenate(parts, axis=1)

    n_chunks = tc // CHUNK
    heads = [slice(h * dh, (h + 1) * dh) for h in range(nh)]
    d = []
    for c in range(n_chunks):
        rows = slice(c * CHUNK, (c + 1) * CHUNK)
        g_rows = g_c[rows]
        b_rows = beta_c[rows]
        acum = _sel_dot(l_incl, _dot_sel(g_rows, spread_g))
        cumw = _sel_dot(l_incl, _dot_sel(g_rows, spread_gw))
        beta = _dot_sel(b_rows, spread_b)
        betaw = _dot_sel(b_rows, spread_bw)
        g_row = jnp.concatenate([g_r[nh + h:nh + h + 1, rows] for h in range(nh)], axis=1)
        crow = _dot_sel(jnp.broadcast_to(g_row, (SUBLANE, gw)), u_bd)[0:1, :]
        decay = jnp.where(ls_incl, jnp.exp(jnp.where(ls_incl, acum - crow, 0.0)), 0.0)
        qc = l2norm_heads(qkv[rows, 0:w]) * (dh ** -0.5)
        kc = l2norm_heads(qkv[rows, w:2 * w])
        vc = qkv[rows, 2 * w:3 * w]
        kq = jnp.concatenate([kc, qc], axis=0).astype(BF16)
        m = _dg(kq, bd_wide(kc).astype(BF16), NT)
        n = -jnp.where(ls_strict, m[:CHUNK] * decay * beta, 0.0)
        d.append(dict(n=n, t=eye_ls + n, p=m[CHUNK:] * decay, cumw=cumw, betaw=betaw, qc=qc, kc=kc, vc=vc))
    for _ in range(CHUNK.bit_length() - 2):
        for e in d:
            e['n'] = _dot3(e['n'], bd(e['n']))
        for e in d:
            e['t'] = e['t'] + _dot3(e['t'], bd(e['n']))
    for e in d:
        e_c = jnp.exp(e['cumw'])
        clast = e['cumw'][CHUNK - 1:CHUNK, :]
        e['u'] = _dot3(e['t'], bd_wide(e['vc'] * e['betaw']))
        e['wm'] = _dot3(e['t'], bd_wide(e['kc'] * (e['betaw'] * e_c)))
        e['q_dec'] = e['qc'] * e_c
        e['k_dec'] = e['kc'] * jnp.exp(clast - e['cumw'])
        e['g'] = jnp.exp(clast)
    for e in d:
        e['qe'] = e['q_dec'] - _bdot(e['p'], bd_wide(e['wm']))
        e['o0'] = _bdot(e['p'], bd_wide(e['u']))
        e['kw'] = [_bdg(e['k_dec'][:, ln], e['wm'][:, ln], TN) for ln in heads]
        e['n0'] = [_bdg(e['k_dec'][:, ln], e['u'][:, ln], TN) for ln in heads]
    for c, e in enumerate(d):
        outs = []
        for h, ln in enumerate(heads):
            s0 = state[h]
            sb = s0.astype(BF16)
            outs.append(_dot(e['qe'][:, ln].astype(BF16), sb) + e['o0'][:, ln])
            state[h] = s0 * e['g'][:, ln] - _dot(e['kw'][h].astype(BF16), sb) + e['n0'][h]
        o_scr[c * CHUNK:(c + 1) * CHUNK, :] = jnp.concatenate(outs, axis=1)

    zs = _silu(z_ref[...])
    for h in range(nh):
        ln = slice(h * dh, (h + 1) * dh)
        o = o_scr[:, ln]
        o = o * lax.rsqrt(jnp.mean(o * o, axis=-1, keepdims=True) + EPS_RMS) * nw_ref[...]
        o_ref[:, ln] = o * zs[:, ln]


def _gdn(y_qkv, y_z, y_ba, params, seq, tc=256):
    t = y_qkv.shape[0]
    tc = min(tc, seq)
    per_b = seq // tc
    nb = t // seq
    w = GDN_WIDTH
    row = lambda b, s: (b * per_b + s, 0)
    full = lambda b, s: (0, 0)
    in_specs = [pl.BlockSpec((tc, GDN_QKV), row), pl.BlockSpec((tc, w), row), pl.BlockSpec((tc, LANE), row)]
    in_specs += [pl.BlockSpec(p.shape, full) for p in params]
    return pl.pallas_call(
        functools.partial(_gdn_kernel, tc=tc),
        grid=(nb, per_b),
        in_specs=in_specs,
        out_specs=pl.BlockSpec((tc, w), row),
        out_shape=jax.ShapeDtypeStruct((t, w), F32),
        scratch_shapes=[pltpu.VMEM((GDN_HEADS, GDN_HEAD_DIM, GDN_HEAD_DIM), F32),
                        pltpu.VMEM((tc + SUBLANE, GDN_QKV), F32),
                        pltpu.VMEM((tc, w), F32)],
        compiler_params=pltpu.CompilerParams(
            dimension_semantics=("parallel", "arbitrary"), vmem_limit_bytes=VMEM_LIMIT),
        name="gdn",
    )(y_qkv, y_z, y_ba, *params)


def _outproj_kernel(x_ref, gt_ref, or_ref, og_ref, w1_ref, w2_ref, o_ref):
    mixed = _dot(or_ref[...].astype(BF16), w1_ref[...]) + _dot(og_ref[...].astype(BF16), w2_ref[...])
    o_ref[...] = x_ref[...] + gt_ref[0] * mixed


def _outproj(x, gate, o_r, o_g, w1, w2, seq, tm=512):
    t, d = x.shape
    tm = min(tm, seq)
    per_b = seq // tm
    row = lambda i: (i, 0)
    full = lambda i: (0, 0)
    return pl.pallas_call(
        _outproj_kernel,
        grid=(t // tm,),
        in_specs=[pl.BlockSpec((tm, d), row),
                  pl.BlockSpec((1, 1, d), lambda i: (i // per_b, 0, 0)),
                  pl.BlockSpec((tm, o_r.shape[1]), row), pl.BlockSpec((tm, o_g.shape[1]), row),
                  pl.BlockSpec(w1.shape, full), pl.BlockSpec(w2.shape, full)],
        out_specs=pl.BlockSpec((tm, d), row),
        out_shape=jax.ShapeDtypeStruct((t, d), F32),
        compiler_params=pltpu.CompilerParams(
            dimension_semantics=("parallel",), vmem_limit_bytes=VMEM_LIMIT),
        name="outproj",
    )(x, gate, o_r, o_g, w1, w2)


def _pad_cols(m, width):
    return jnp.pad(m, ((0, 0), (0, width - m.shape[1])))


def kernel(x, c, norm_gain, ada_w, ada_b, ffn_w_gu, ffn_w_down, w_in, w_out, rwkv_mu, rwkv_w0, rwkv_w_up, rwkv_a0, rwkv_a_up, rwkv_g_up, rwkv_k_k, rwkv_k_a, rwkv_r_k, rwkv_gn_w, rwkv_gn_b, vres_w_down, vres_mu, vres_w_up, vres_v0, gdn_conv_w, gdn_a_log, gdn_dt_bias, gdn_norm_w, final_gain):
    bsz, seq, d = x.shape
    depth = norm_gain.shape[0]
    nh = GDN_HEADS
    mod = _adaln(c, ada_w, ada_b)
    xf = x.reshape(bsz * seq, d)
    v_first = None
    row = lambda p: p.reshape(1, -1)
    for l in range(depth):
        def mods(sub):
            m = mod[l * 3 + sub]
            return tuple(m[:, i * d:(i + 1) * d].reshape(bsz, 1, d) for i in range(3))

        shift, scale, gate = mods(0)
        xf = _ffn(xf, norm_gain[l, 0], shift, scale, gate,
                  ffn_w_gu[l, 0].astype(BF16), ffn_w_down[l, 0].astype(BF16), seq)

        shift, scale, gate = mods(1)
        wl = w_in[l]
        w_ba = _pad_cols(wl[:, RWKV_COLS + 4 * GDN_WIDTH:], LANE)
        weights = [wl[:, :RWKV_COLS], wl[:, RWKV_COLS:RWKV_COLS + GDN_QKV],
                   wl[:, RWKV_COLS + GDN_QKV:RWKV_COLS + 4 * GDN_WIDTH], w_ba]
        if l > 0:
            weights.append(_pad_cols(vres_w_down[l - 1], LANE))
        ys = _inproj(xf, norm_gain[l, 1], shift, scale, [wt.astype(BF16) for wt in weights], seq)
        y_r, y_qkv, y_z, y_ba = ys[:4]

        rparams = [row(rwkv_mu[l]), row(rwkv_w0[l]), rwkv_w_up[l], row(rwkv_a0[l]), rwkv_a_up[l],
                   rwkv_g_up[l], row(rwkv_k_k[l]), row(rwkv_k_a[l]), row(rwkv_r_k[l]),
                   row(rwkv_gn_w[l]), row(rwkv_gn_b[l])]
        if l == 0:
            o_r, v_first = _rwkv(y_r, rparams, seq)
        else:
            vparams = [_pad_cols(row(vres_mu[l - 1]), LANE),
                       jnp.pad(vres_w_up[l - 1], ((0, LANE - VRES_RANK), (0, 0))),
                       row(vres_v0[l - 1])]
            (o_r,) = _rwkv(y_r, rparams, seq, y_vres=ys[4], v_first=v_first, vres_params=vparams)

        alog = gdn_a_log[l].astype(F32)
        dtb = gdn_dt_bias[l].astype(F32)
        place = lambda p: jnp.pad(p, (nh, LANE - 2 * nh))
        gparams = [gdn_conv_w[l], row(place(alog)), row(place(dtb)),
                   place(alog)[:SUBLANE].reshape(SUBLANE, 1), place(dtb)[:SUBLANE].reshape(SUBLANE, 1),
                   row(gdn_norm_w[l])]
        o_g = _gdn(y_qkv, y_z, y_ba, gparams, seq)

        wo = w_out[l].astype(BF16)
        xf = _outproj(xf, gate, o_r, o_g, wo[:RWKV_WIDTH], wo[RWKV_WIDTH:], seq)

        shift, scale, gate = mods(2)
        xf = _ffn(xf, norm_gain[l, 2], shift, scale, gate,
                  ffn_w_gu[l, 1].astype(BF16), ffn_w_down[l, 1].astype(BF16), seq,
                  final_gain=final_gain if l == depth - 1 else None)
    return xf.reshape(bsz, seq, d)
```

```python
import functools

import jax
import jax.numpy as jnp
from jax import lax
from jax.experimental import pallas as pl
from jax.experimental.pallas import tpu as pltpu

F32 = jnp.float32
BF16 = jnp.bfloat16
HI = lax.Precision.HIGHEST

RWKV_HEAD_DIM = 64
RWKV_HEADS = 8
RWKV_WIDTH = RWKV_HEAD_DIM * RWKV_HEADS
GDN_HEAD_DIM = 128
GDN_HEADS = 4
GDN_WIDTH = GDN_HEAD_DIM * GDN_HEADS
DECAY_RANK = 64
ICLR_RANK = 64
VRES_RANK = 32
GATE_RANK = 128
RWKV_COLS = 3 * RWKV_WIDTH + DECAY_RANK + ICLR_RANK + GATE_RANK
GDN_QKV = 3 * GDN_WIDTH
CONV_WIDTH = 4
CHUNK = 64
LANE = 128
SUBLANE = 8
MXU_TILE = 256
assert CHUNK == RWKV_HEAD_DIM and MXU_TILE % RWKV_HEAD_DIM == 0 and RWKV_WIDTH % MXU_TILE == 0
assert GDN_HEADS * CHUNK == MXU_TILE
EPS_RMS = 1e-6
EPS_GN = 64e-5
EPS_L2 = 1e-6
MACARON_WEIGHT = 0.5
VMEM_LIMIT = 52 * 1024 * 1024

NT = (((1,), (1,)), ((), ()))
TN = (((0,), (0,)), ((), ()))


def _sigmoid(x):
    return 1.0 / (1.0 + jnp.exp(-x))


def _softplus(x):
    return jnp.maximum(x, 0.0) + jnp.log(1.0 + jnp.exp(-jnp.abs(x)))


def _silu(x):
    return x * _sigmoid(x)


def _dot(a, b, precision=None):
    return jnp.dot(a, b, preferred_element_type=F32, precision=precision)


def _dg(a, b, dims, precision=None):
    return lax.dot_general(a, b, dims, preferred_element_type=F32, precision=precision)


def _bdg(a, b, dims):
    return lax.dot_general(a.astype(BF16), b.astype(BF16), dims, preferred_element_type=F32)


def _split2(a):
    hi = a.astype(BF16)
    return hi, (a - hi.astype(F32)).astype(BF16)


def _stack_bd(xb, mask, reps):
    return jnp.where(mask, jnp.concatenate([xb] * reps, axis=0), jnp.zeros((), BF16))


def _dot3_bd(a, y, stack):
    ah, al = _split2(a)
    yh, yl = _split2(y)
    bh, bl = stack(yh), stack(yl)
    d = lambda p, q: jnp.dot(p, q, preferred_element_type=F32)
    return d(ah, bh) + (d(ah, bl) + d(al, bh))


def _split3(a):
    hi = a.astype(BF16)
    r1 = a - hi.astype(F32)
    mid = r1.astype(BF16)
    return hi, mid, (r1 - mid.astype(F32)).astype(BF16)


def _sel_dot(sel, x):
    sb = sel.astype(BF16)
    hi, mid, lo = _split3(x)
    d = lambda y: jnp.dot(sb, y, preferred_element_type=F32)
    return d(hi) + (d(mid) + d(lo))


def _dot_sel(x, sel):
    sb = sel.astype(BF16)
    hi, mid, lo = _split3(x)
    d = lambda y: jnp.dot(y, sb, preferred_element_type=F32)
    return d(hi) + (d(mid) + d(lo))


def _dot_sel2(x, sel):
    sb = sel.astype(BF16)
    hi, lo = _split2(x)
    return jnp.dot(hi, sb, preferred_element_type=F32) + jnp.dot(lo, sb, preferred_element_type=F32)


def _norm_mod(x, gain, shift, scale):
    ms = jnp.mean(x * x, axis=-1, keepdims=True)
    y = x * lax.rsqrt(ms + EPS_RMS) * gain
    return y * (1.0 + scale) + shift


def _tri(n, strict):
    r = lax.broadcasted_iota(jnp.int32, (n, n), 0)
    c = lax.broadcasted_iota(jnp.int32, (n, n), 1)
    return (r > c) if strict else (r >= c)


def _mod_kernel(c_ref, w_ref, b_ref, o_ref):
    ca = _silu(c_ref[...])
    o_ref[0] = _dot(ca, w_ref[0], HI) + b_ref[0]


def _adaln(c, ada_w, ada_b):
    n_l, n_sub, d, d3 = ada_w.shape
    b = c.shape[0]
    rows = -(-b // SUBLANE) * SUBLANE
    c_pad = jnp.pad(c, ((0, rows - b), (0, 0)))
    tn = d
    out = pl.pallas_call(
        _mod_kernel,
        grid=(n_l * n_sub, d3 // tn),
        in_specs=[
            pl.BlockSpec((rows, d), lambda i, j: (0, 0)),
            pl.BlockSpec((1, d, tn), lambda i, j: (i, 0, j)),
            pl.BlockSpec((1, 1, tn), lambda i, j: (i, 0, j)),
        ],
        out_specs=pl.BlockSpec((1, rows, tn), lambda i, j: (i, 0, j)),
        out_shape=jax.ShapeDtypeStruct((n_l * n_sub, rows, d3), F32),
        compiler_params=pltpu.CompilerParams(
            dimension_semantics=("parallel", "parallel"), vmem_limit_bytes=VMEM_LIMIT),
        name="adaln_mod",
    )(c_pad, ada_w.reshape(n_l * n_sub, d, d3), ada_b.reshape(n_l * n_sub, 1, d3))
    return out[:, :b, :]


def _ffn_kernel(*refs, final, mix, sub):
    refs = list(refs)
    x_ref, gain_ref, sh_ref, sc_ref, gt_ref, wg_ref, wu_ref, wd_ref = refs[:8]
    del refs[:8]
    if mix:
        gm_ref, or_ref, og_ref, w1_ref, w2_ref = refs[:5]
        del refs[:5]
    if final:
        fg_ref = refs.pop(0)
    (o_ref,) = refs

    x = x_ref[...]
    if mix:
        mixed = _dot(or_ref[...].astype(BF16), w1_ref[...]) + _dot(og_ref[...].astype(BF16), w2_ref[...])
        x = x + gm_ref[0] * mixed
    h = _norm_mod(x, gain_ref[...], sh_ref[0], sc_ref[0]).astype(BF16)

    f = wd_ref.shape[0]
    blocks = [(a, min(a + sub, f)) for a in range(0, f, sub)]
    gu = [(_dot(h, wg_ref[:, a:b]), _dot(h, wu_ref[:, a:b])) for a, b in blocks]
    down = None
    for (a, b), (g, u) in zip(blocks, gu):
        part = _dot((_silu(g) * u).astype(BF16), wd_ref[a:b, :])
        down = part if down is None else down + part

    y = x + (MACARON_WEIGHT * gt_ref[0]) * down
    if final:
        ms = jnp.mean(y * y, axis=-1, keepdims=True)
        y = y * lax.rsqrt(ms + EPS_RMS) * fg_ref[...]
    o_ref[...] = y


def _ffn(x, gain, shift, scale, gate, w_g, w_u, w_dn, seq, final_gain=None, mix=None, tm=512,
         sub=3 * MXU_TILE):
    t, d = x.shape
    tm = min(tm, seq)
    per_b = seq // tm
    final = final_gain is not None
    row = lambda i: (i, 0)
    once = dict(index_map=lambda i: (0, 0), pipeline_mode=pl.Buffered(1))
    modspec = pl.BlockSpec((1, 1, d), lambda i: (i // per_b, 0, 0))
    in_specs = [
        pl.BlockSpec((tm, d), row),
        pl.BlockSpec((1, d), **once),
        modspec, modspec, modspec,
        pl.BlockSpec(w_g.shape, **once), pl.BlockSpec(w_u.shape, **once), pl.BlockSpec(w_dn.shape, **once),
    ]
    args = [x, gain.reshape(1, d), shift, scale, gate, w_g, w_u, w_dn]
    if mix is not None:
        gate_mix, o_r, o_g, w1, w2 = mix
        in_specs += [modspec, pl.BlockSpec((tm, o_r.shape[1]), row), pl.BlockSpec((tm, o_g.shape[1]), row),
                     pl.BlockSpec(w1.shape, **once), pl.BlockSpec(w2.shape, **once)]
        args += [gate_mix, o_r, o_g, w1, w2]
    if final:
        in_specs.append(pl.BlockSpec((1, d), **once))
        args.append(final_gain.reshape(1, d))
    return pl.pallas_call(
        functools.partial(_ffn_kernel, final=final, mix=mix is not None, sub=sub),
        grid=(t // tm,),
        in_specs=in_specs,
        out_specs=pl.BlockSpec((tm, d), row),
        out_shape=jax.ShapeDtypeStruct((t, d), F32),
        compiler_params=pltpu.CompilerParams(
            dimension_semantics=("parallel",), vmem_limit_bytes=VMEM_LIMIT),
        name="ffn" + ("_mix" if mix is not None else "") + ("_final" if final else ""),
    )(*args)


def _inproj_kernel(x_ref, gain_ref, sh_ref, sc_ref, *rest):
    n = len(rest) // 2
    w_refs, o_refs = rest[:n], rest[n:]
    h = _norm_mod(x_ref[...], gain_ref[...], sh_ref[0], sc_ref[0]).astype(BF16)
    for w_ref, o_ref in zip(w_refs, o_refs):
        o_ref[...] = _dot(h, w_ref[...])


def _inproj(x, gain, shift, scale, weights, seq, tm=512):
    t, d = x.shape
    tm = min(tm, seq)
    per_b = seq // tm
    modspec = pl.BlockSpec((1, 1, d), lambda i: (i // per_b, 0, 0))
    in_specs = [pl.BlockSpec((tm, d), lambda i: (i, 0)), pl.BlockSpec((1, d), lambda i: (0, 0)),
                modspec, modspec]
    in_specs += [pl.BlockSpec(w.shape, lambda i: (0, 0)) for w in weights]
    out_specs = [pl.BlockSpec((tm, w.shape[1]), lambda i: (i, 0)) for w in weights]
    out_shape = [jax.ShapeDtypeStruct((t, w.shape[1]), F32) for w in weights]
    return pl.pallas_call(
        _inproj_kernel,
        grid=(t // tm,),
        in_specs=in_specs,
        out_specs=out_specs,
        out_shape=out_shape,
        compiler_params=pltpu.CompilerParams(
            dimension_semantics=("parallel",), vmem_limit_bytes=VMEM_LIMIT),
        name="inproj",
    )(x, gain.reshape(1, d), shift, scale, *weights)


def _shift_rows(y, carry_row):
    rolled = pltpu.roll(y, 1, 0)
    first = lax.broadcasted_iota(jnp.int32, y.shape, 0) == 0
    return jnp.where(first, carry_row, rolled)


def _rwkv_kernel(*refs, has_vres, tc):
    if has_vres:
        (y_ref, yv_ref, vf_ref, mu_ref, w0_ref, wup_ref, a0_ref, aup_ref, gup_ref, kk_ref, ka_ref,
         rk_ref, gnw_ref, gnb_ref, vmu_ref, vup_ref, v0_ref,
         o_ref, state, carry, carry_v, o_scr) = refs
    else:
        (y_ref, mu_ref, w0_ref, wup_ref, a0_ref, aup_ref, gup_ref, kk_ref, ka_ref,
         rk_ref, gnw_ref, gnb_ref,
         o_ref, v_ref, state, carry, o_scr) = refs
    n, hd, w = RWKV_HEAD_DIM, RWKV_HEADS, RWKV_WIDTH

    @pl.when(pl.program_id(1) == 0)
    def _():
        state[...] = jnp.zeros_like(state)
        carry[...] = jnp.zeros_like(carry)
        if has_vres:
            carry_v[...] = jnp.zeros_like(carry_v)

    y = y_ref[...]
    prev = _shift_rows(y, carry[0:1, :])
    carry[0:1, :] = y[tc - 1:tc, :]
    ym = y + (prev - y) * mu_ref[...]
    r = ym[:, 0:w]
    k = ym[:, w:2 * w]
    v = ym[:, 2 * w:3 * w]
    wd = ym[:, 3 * w:3 * w + DECAY_RANK]
    ad = ym[:, 3 * w + DECAY_RANK:3 * w + DECAY_RANK + ICLR_RANK]
    gd = ym[:, 3 * w + DECAY_RANK + ICLR_RANK:]

    if has_vres:
        yv = yv_ref[...]
        prev_v = _shift_rows(yv, carry_v[0:1, :])
        carry_v[0:1, :] = yv[tc - 1:tc, :]
        yvm = yv + (prev_v - yv) * vmu_ref[...]
        logit = v0_ref[...] + _dot(yvm.astype(BF16), vup_ref[...].astype(BF16))
        v = v + (vf_ref[...] - v) * _sigmoid(logit)
    else:
        v_ref[...] = v

    w_log = -_softplus(-(w0_ref[...] + _dot(jnp.tanh(wd).astype(BF16), wup_ref[...].astype(BF16)))) - 0.5
    logw = -jnp.exp(w_log)
    a = _sigmoid(a0_ref[...] + _dot(ad.astype(BF16), aup_ref[...].astype(BF16)))
    g = _dot(_sigmoid(gd).astype(BF16), gup_ref[...].astype(BF16))

    gw = MXU_TILE
    hpg = gw // n
    groups = [slice(gi * gw, (gi + 1) * gw) for gi in range(w // gw)]
    rr = lax.broadcasted_iota(jnp.int32, (gw, gw), 0)
    cc = lax.broadcasted_iota(jnp.int32, (gw, gw), 1)
    same_blk = (rr // CHUNK) == (cc // CHUNK)
    head_ones = (rr // n) == (cc // n)
    tt = lax.broadcasted_iota(jnp.int32, (CHUNK, gw), 0)
    ss = lax.broadcasted_iota(jnp.int32, (CHUNK, gw), 1) % CHUNK
    ls_strict = tt > ss
    ls_incl = tt >= ss
    eye_ls = (tt == ss).astype(F32)
    l_incl = _tri(CHUNK, False)

    def head_sum(x):
        return jnp.concatenate([_dot_sel2(x[:, ln], head_ones) for ln in groups], axis=1)

    def bd(x):
        return _stack_bd(x.astype(BF16), same_blk, hpg)

    def ls_dot(a, b_bd):
        return jnp.dot(a.astype(BF16), b_bd, preferred_element_type=F32)

    kkr = k * kk_ref[...]
    kk = kkr * lax.rsqrt(head_sum(kkr * kkr) + EPS_L2)
    k2 = k * (1.0 + (a - 1.0) * ka_ref[...])
    b = kk * a
    av = -kk

    n_chunks = tc // CHUNK
    insts = [(c, gi) for c in range(n_chunks) for gi in range(len(groups))]
    d = {}
    for c in range(n_chunks):
        rows = slice(c * CHUNK, (c + 1) * CHUNK)
        lw = logw[rows]
        cum = _sel_dot(l_incl, lw)
        last = cum[CHUNK - 1:CHUNK, :]
        e_pos = jnp.exp(cum)
        e_neg = jnp.exp(-cum)
        e_to_end = jnp.exp(last - cum)
        gam_end = jnp.exp(last)
        at = av[rows] * jnp.exp(cum - lw)
        rt = r[rows] * e_pos
        bt = b[rows] * e_neg
        kt = k2[rows] * e_neg
        bg = b[rows] * e_to_end
        kg = k2[rows] * e_to_end
        vc = v[rows]
        for gi, ln in enumerate(groups):
            d[c, gi] = dict(at=at[:, ln], rt=rt[:, ln], bt=bt[:, ln], kt=kt[:, ln], bg=bg[:, ln],
                            kg=kg[:, ln], v=vc[:, ln], gam=gam_end[:, ln])
    for i in insts:
        e = d[i]
        x_ar = jnp.concatenate([e['at'], e['rt']], axis=0).astype(BF16)
        y_bk = jnp.concatenate([bd(e['bt']), bd(e['kt'])], axis=0)
        m1 = _dg(x_ar, y_bk, NT)
        e['n'] = jnp.where(ls_strict, m1[:CHUNK, :gw], 0.0)
        e['a_ak'] = jnp.where(ls_strict, m1[:CHUNK, gw:], 0.0)
        e['a_rb'] = jnp.where(ls_incl, m1[CHUNK:, :gw], 0.0)
        e['a_rk'] = jnp.where(ls_incl, m1[CHUNK:, gw:], 0.0)
        e['t'] = eye_ls + e['n']
        e['v_bd'] = bd(e['v'])
    for _ in range(CHUNK.bit_length() - 2):
        for i in insts:
            d[i]['n'] = ls_dot(d[i]['n'], bd(d[i]['n']))
        for i in insts:
            d[i]['t'] = d[i]['t'] + ls_dot(d[i]['t'], bd(d[i]['n']))
    for i in insts:
        e = d[i]
        e['w'] = ls_dot(e['t'], bd(e['at']))
        e['akv'] = ls_dot(e['a_ak'], e['v_bd'])
    for i in insts:
        e = d[i]
        e['u0'] = ls_dot(e['t'], bd(e['akv']))
        e['q'] = e['rt'] + ls_dot(e['a_rb'], bd(e['w']))
    for i in insts:
        e = d[i]
        a_r = jnp.concatenate([e['a_rb'], e['a_rk']], axis=1)
        e['o0'] = ls_dot(a_r, jnp.concatenate([bd(e['u0']), e['v_bd']], axis=0))
        e['m'] = jnp.where(same_blk, _bdg(e['w'], e['bg'], TN), 0.0)
        uv = jnp.concatenate([e['u0'], e['v']], axis=0)
        bk_end = jnp.concatenate([e['bg'], e['kg']], axis=0)
        e['n0'] = jnp.where(same_blk, _bdg(uv, bk_end, TN), 0.0)
    for c, gi in insts:
        e = d[c, gi]
        s0 = state[gi]
        sb = s0.astype(BF16)
        o_scr[c * CHUNK:(c + 1) * CHUNK, groups[gi]] = _dg(e['q'].astype(BF16), sb, NT) + e['o0']
        state[gi] = s0 * e['gam'] + jnp.dot(sb, e['m'].astype(BF16), preferred_element_type=F32) + e['n0']

    o = o_scr[...]
    inv_n = 1.0 / n
    mean = head_sum(o) * inv_n
    dev = o - mean
    var = head_sum(dev * dev) * inv_n
    on = dev * lax.rsqrt(var + EPS_GN) * gnw_ref[...] + gnb_ref[...]
    bonus = head_sum(r * k2 * rk_ref[...])
    o_ref[...] = (on + bonus * v) * g


def _rwkv(y_r, params, seq, y_vres=None, v_first=None, vres_params=None, tc=512):
    t = y_r.shape[0]
    tc = min(tc, seq)
    per_b = seq // tc
    nb = t // seq
    has_vres = y_vres is not None
    w = RWKV_WIDTH
    row = lambda b, s: (b * per_b + s, 0)
    full = lambda b, s: (0, 0)
    in_specs = [pl.BlockSpec((tc, RWKV_COLS), row)]
    args = [y_r]
    if has_vres:
        in_specs += [pl.BlockSpec((tc, LANE), row), pl.BlockSpec((tc, w), row)]
        args += [y_vres, v_first]
    plist = list(params) + (list(vres_params) if has_vres else [])
    in_specs += [pl.BlockSpec(p.shape, full) for p in plist]
    args += plist
    out_specs = [pl.BlockSpec((tc, w), row)]
    out_shape = [jax.ShapeDtypeStruct((t, w), F32)]
    scratch = [pltpu.VMEM((RWKV_WIDTH // MXU_TILE, MXU_TILE, MXU_TILE), F32),
               pltpu.VMEM((SUBLANE, RWKV_COLS), F32)]
    if has_vres:
        scratch.append(pltpu.VMEM((SUBLANE, LANE), F32))
    else:
        out_specs.append(pl.BlockSpec((tc, w), row))
        out_shape.append(jax.ShapeDtypeStruct((t, w), F32))
    scratch.append(pltpu.VMEM((tc, w), F32))
    return pl.pallas_call(
        functools.partial(_rwkv_kernel, has_vres=has_vres, tc=tc),
        grid=(nb, per_b),
        in_specs=in_specs,
        out_specs=out_specs,
        out_shape=out_shape,
        scratch_shapes=scratch,
        compiler_params=pltpu.CompilerParams(
            dimension_semantics=("parallel", "arbitrary"), vmem_limit_bytes=VMEM_LIMIT),
        name="rwkv7_vres" if has_vres else "rwkv7",
    )(*args)


def _gdn_kernel(qkv_ref, z_ref, ba_ref, cw_ref, alog_r_ref, dt_r_ref, alog_c_ref, dt_c_ref, nw_ref,
                o_ref, state, ext, o_scr, *, tc):
    dh, nh, w = GDN_HEAD_DIM, GDN_HEADS, GDN_WIDTH
    pad = SUBLANE

    @pl.when(pl.program_id(1) == 0)
    def _():
        state[...] = jnp.zeros_like(state)
        ext[0:pad, :] = jnp.zeros((pad, GDN_QKV), F32)

    x = qkv_ref[...]
    ext[pad:pad + tc, :] = x
    conv = x * cw_ref[CONV_WIDTH - 1:CONV_WIDTH, :]
    for j in range(CONV_WIDTH - 1):
        back = CONV_WIDTH - 1 - j
        conv = conv + ext[pad - back:pad - back + tc, :] * cw_ref[j:j + 1, :]
    ext[0:pad, :] = x[tc - pad:tc, :]
    qkv = _silu(conv)

    ba = ba_ref[...]
    beta_c = _sigmoid(ba)
    g_c = -jnp.exp(alog_r_ref[...]) * _softplus(ba + dt_r_ref[...])
    er = lax.broadcasted_iota(jnp.int32, (pad, LANE), 0)
    ec = lax.broadcasted_iota(jnp.int32, (pad, LANE), 1)
    pick = (er == ec).astype(BF16)
    ba_hi, ba_mid, ba_lo = _split3(ba)
    ba_t = _dg(pick, ba_hi, NT) + (_dg(pick, ba_mid, NT) + _dg(pick, ba_lo, NT))
    g_r = -jnp.exp(alog_c_ref[...]) * _softplus(ba_t + dt_c_ref[...])

    gw = nh * CHUNK
    rr = lax.broadcasted_iota(jnp.int32, (gw, gw), 0)
    cc = lax.broadcasted_iota(jnp.int32, (gw, gw), 1)
    same_blk = (rr // CHUNK) == (cc // CHUNK)
    u_bd = jnp.logical_and(same_blk, rr <= cc)
    tt = lax.broadcasted_iota(jnp.int32, (CHUNK, gw), 0)
    ss = lax.broadcasted_iota(jnp.int32, (CHUNK, gw), 1) % CHUNK
    ls_strict = tt > ss
    ls_incl = tt >= ss
    eye_ls = (tt == ss).astype(F32)
    l_incl = _tri(CHUNK, False)
    src = lax.broadcasted_iota(jnp.int32, (LANE, gw), 0)
    spread_g = src == nh + lax.broadcasted_iota(jnp.int32, (LANE, gw), 1) // CHUNK
    spread_b = src == lax.broadcasted_iota(jnp.int32, (LANE, gw), 1) // CHUNK
    srcw = lax.broadcasted_iota(jnp.int32, (LANE, w), 0)
    spread_gw = srcw == nh + lax.broadcasted_iota(jnp.int32, (LANE, w), 1) // dh
    spread_bw = srcw == lax.broadcasted_iota(jnp.int32, (LANE, w), 1) // dh
    head_rows = (lax.broadcasted_iota(jnp.int32, (gw, w), 0) // CHUNK
                 == lax.broadcasted_iota(jnp.int32, (gw, w), 1) // dh)

    def bd(xb):
        return _stack_bd(xb, same_blk, nh)

    def bd_wide(xb):
        return _stack_bd(xb, head_rows, nh)

    def ls_dot(a, y):
        return jnp.dot(a.astype(BF16), bd(y.astype(BF16)), preferred_element_type=F32)

    def l2norm_heads(x):
        parts = []
        for h in range(nh):
            xh = x[:, h * dh:(h + 1) * dh]
            parts.append(xh * lax.rsqrt(jnp.sum(xh * xh, axis=-1, keepdims=True) + EPS_L2))
        return jnp.concatenate(parts, axis=1)

    n_chunks = tc // CHUNK
    heads = [slice(h * dh, (h + 1) * dh) for h in range(nh)]
    d = []
    for c in range(n_chunks):
        rows = slice(c * CHUNK, (c + 1) * CHUNK)
        g_rows = g_c[rows]
        b_rows = beta_c[rows]
        cum_c = _sel_dot(l_incl, g_rows)
        acum = _dot_sel(cum_c, spread_g)
        cumw = _dot_sel(cum_c, spread_gw)
        beta = _dot_sel(b_rows, spread_b)
        betaw = _dot_sel(b_rows, spread_bw)
        g_row = jnp.concatenate([g_r[nh + h:nh + h + 1, rows] for h in range(nh)], axis=1)
        crow = _dot_sel(jnp.broadcast_to(g_row, (SUBLANE, gw)), u_bd)[0:1, :]
        decay = jnp.where(ls_incl, jnp.exp(jnp.where(ls_incl, acum - crow, 0.0)), 0.0)
        qc = l2norm_heads(qkv[rows, 0:w]) * (dh ** -0.5)
        kc = l2norm_heads(qkv[rows, w:2 * w])
        vc = qkv[rows, 2 * w:3 * w]
        kq = jnp.concatenate([kc, qc], axis=0).astype(BF16)
        m = _dg(kq, bd_wide(kc.astype(BF16)), NT)
        n = -jnp.where(ls_strict, m[:CHUNK] * decay * beta, 0.0)
        d.append(dict(n1=n, n=n, t=eye_ls + n, p=m[CHUNK:] * decay, cumw=cumw, betaw=betaw, qc=qc, kc=kc, vc=vc))
    for _ in range(CHUNK.bit_length() - 2):
        for e in d:
            e['n'] = ls_dot(e['n'], e['n'])
        for e in d:
            e['t'] = e['t'] + ls_dot(e['t'], e['n'])
    for e in d:
        e['r'] = eye_ls - e['t'] + _dot3_bd(e['n1'], e['t'], bd)
    for e in d:
        e['t'] = e['t'] + ls_dot(e['t'], e['r'])
    for e in d:
        e_c = jnp.exp(e['cumw'])
        clast = e['cumw'][CHUNK - 1:CHUNK, :]
        tb = e['t'].astype(BF16)
        e['u'] = _dot(tb, bd_wide((e['vc'] * e['betaw']).astype(BF16)))
        e['wm'] = _dot(tb, bd_wide((e['kc'] * (e['betaw'] * e_c)).astype(BF16)))
        e['q_dec'] = e['qc'] * e_c
        e['k_dec'] = e['kc'] * jnp.exp(clast - e['cumw'])
        e['g'] = jnp.exp(clast)
    for e in d:
        pb = e['p'].astype(BF16)
        e['qe'] = e['q_dec'] - _dot(pb, bd_wide(e['wm'].astype(BF16)))
        e['o0'] = _dot(pb, bd_wide(e['u'].astype(BF16)))
        e['kw'] = [_bdg(e['k_dec'][:, ln], e['wm'][:, ln], TN) for ln in heads]
        e['n0'] = [_bdg(e['k_dec'][:, ln], e['u'][:, ln], TN) for ln in heads]
    for c, e in enumerate(d):
        outs = []
        for h, ln in enumerate(heads):
            s0 = state[h]
            sb = s0.astype(BF16)
            outs.append(_dot(e['qe'][:, ln].astype(BF16), sb) + e['o0'][:, ln])
            state[h] = s0 * e['g'][:, ln] - _dot(e['kw'][h].astype(BF16), sb) + e['n0'][h]
        o_scr[c * CHUNK:(c + 1) * CHUNK, :] = jnp.concatenate(outs, axis=1)

    zs = _silu(z_ref[...])
    for h in range(nh):
        ln = slice(h * dh, (h + 1) * dh)
        o = o_scr[:, ln]
        o = o * lax.rsqrt(jnp.mean(o * o, axis=-1, keepdims=True) + EPS_RMS) * nw_ref[...]
        o_ref[:, ln] = o * zs[:, ln]


def _gdn(y_qkv, y_z, y_ba, params, seq, tc=256):
    t = y_qkv.shape[0]
    tc = min(tc, seq)
    per_b = seq // tc
    nb = t // seq
    w = GDN_WIDTH
    row = lambda b, s: (b * per_b + s, 0)
    full = lambda b, s: (0, 0)
    in_specs = [pl.BlockSpec((tc, GDN_QKV), row), pl.BlockSpec((tc, w), row), pl.BlockSpec((tc, LANE), row)]
    in_specs += [pl.BlockSpec(p.shape, full) for p in params]
    return pl.pallas_call(
        functools.partial(_gdn_kernel, tc=tc),
        grid=(nb, per_b),
        in_specs=in_specs,
        out_specs=pl.BlockSpec((tc, w), row),
        out_shape=jax.ShapeDtypeStruct((t, w), F32),
        scratch_shapes=[pltpu.VMEM((GDN_HEADS, GDN_HEAD_DIM, GDN_HEAD_DIM), F32),
                        pltpu.VMEM((tc + SUBLANE, GDN_QKV), F32),
                        pltpu.VMEM((tc, w), F32)],
        compiler_params=pltpu.CompilerParams(
            dimension_semantics=("parallel", "arbitrary"), vmem_limit_bytes=VMEM_LIMIT),
        name="gdn",
    )(y_qkv, y_z, y_ba, *params)


def _pad_cols(m, width):
    return jnp.pad(m, ((0, 0), (0, width - m.shape[1])))


def kernel(x, c, norm_gain, ada_w, ada_b, ffn_w_gu, ffn_w_down, w_in, w_out, rwkv_mu, rwkv_w0, rwkv_w_up, rwkv_a0, rwkv_a_up, rwkv_g_up, rwkv_k_k, rwkv_k_a, rwkv_r_k, rwkv_gn_w, rwkv_gn_b, vres_w_down, vres_mu, vres_w_up, vres_v0, gdn_conv_w, gdn_a_log, gdn_dt_bias, gdn_norm_w, final_gain):
    bsz, seq, d = x.shape
    depth = norm_gain.shape[0]
    nh = GDN_HEADS
    mod = _adaln(c, ada_w, ada_b)
    xf = x.reshape(bsz * seq, d)
    v_first = None
    row = lambda p: p.reshape(1, -1)
    for l in range(depth):
        def mods(sub):
            m = mod[l * 3 + sub]
            return tuple(m[:, i * d:(i + 1) * d].reshape(bsz, 1, d) for i in range(3))

        shift, scale, gate = mods(0)
        f = ffn_w_down.shape[2]
        wgu = ffn_w_gu[l, 0].astype(BF16)
        xf = _ffn(xf, norm_gain[l, 0], shift, scale, gate, wgu[:, :f], wgu[:, f:],
                  ffn_w_down[l, 0].astype(BF16), seq)

        shift, scale, gate = mods(1)
        wl = w_in[l]
        w_ba = _pad_cols(wl[:, RWKV_COLS + 4 * GDN_WIDTH:], LANE)
        weights = [wl[:, :RWKV_COLS], wl[:, RWKV_COLS:RWKV_COLS + GDN_QKV],
                   wl[:, RWKV_COLS + GDN_QKV:RWKV_COLS + 4 * GDN_WIDTH], w_ba]
        if l > 0:
            weights.append(_pad_cols(vres_w_down[l - 1], LANE))
        ys = _inproj(xf, norm_gain[l, 1], shift, scale, [wt.astype(BF16) for wt in weights], seq)
        y_r, y_qkv, y_z, y_ba = ys[:4]

        rparams = [row(rwkv_mu[l]), row(rwkv_w0[l]), rwkv_w_up[l], row(rwkv_a0[l]), rwkv_a_up[l],
                   rwkv_g_up[l], row(rwkv_k_k[l]), row(rwkv_k_a[l]), row(rwkv_r_k[l]),
                   row(rwkv_gn_w[l]), row(rwkv_gn_b[l])]
        if l == 0:
            o_r, v_first = _rwkv(y_r, rparams, seq)
        else:
            vparams = [_pad_cols(row(vres_mu[l - 1]), LANE),
                       jnp.pad(vres_w_up[l - 1], ((0, LANE - VRES_RANK), (0, 0))),
                       row(vres_v0[l - 1])]
            (o_r,) = _rwkv(y_r, rparams, seq, y_vres=ys[4], v_first=v_first, vres_params=vparams)

        alog = gdn_a_log[l].astype(F32)
        dtb = gdn_dt_bias[l].astype(F32)
        place = lambda p: jnp.pad(p, (nh, LANE - 2 * nh))
        gparams = [gdn_conv_w[l], row(place(alog)), row(place(dtb)),
                   place(alog)[:SUBLANE].reshape(SUBLANE, 1), place(dtb)[:SUBLANE].reshape(SUBLANE, 1),
                   row(gdn_norm_w[l])]
        o_g = _gdn(y_qkv, y_z, y_ba, gparams, seq)

        wo = w_out[l].astype(BF16)
        mix = (gate, o_r, o_g, wo[:RWKV_WIDTH], wo[RWKV_WIDTH:])
        shift, scale, gate = mods(2)
        wgu = ffn_w_gu[l, 1].astype(BF16)
        xf = _ffn(xf, norm_gain[l, 2], shift, scale, gate, wgu[:, :f], wgu[:, f:],
                  ffn_w_down[l, 1].astype(BF16), seq,
                  final_gain=final_gain if l == depth - 1 else None, mix=mix)
    return xf.reshape(bsz, seq, d)
```

```python
import functools
import math

import jax
import jax.numpy as jnp
from jax import lax
from jax.experimental import pallas as pl
from jax.experimental.pallas import tpu as pltpu

F32 = jnp.float32
BF16 = jnp.bfloat16

RWKV_HEAD_DIM = 64
RWKV_HEADS = 8
RWKV_WIDTH = RWKV_HEAD_DIM * RWKV_HEADS
GDN_HEAD_DIM = 128
GDN_HEADS = 4
GDN_WIDTH = GDN_HEAD_DIM * GDN_HEADS
DECAY_RANK = 64
ICLR_RANK = 64
VRES_RANK = 32
GATE_RANK = 128
RWKV_COLS = 3 * RWKV_WIDTH + DECAY_RANK + ICLR_RANK + GATE_RANK
GDN_QKV = 3 * GDN_WIDTH
CONV_WIDTH = 4
CHUNK = 64
LANE = 128
SUBLANE = 8
MXU_TILE = 256
assert CHUNK == RWKV_HEAD_DIM and MXU_TILE % RWKV_HEAD_DIM == 0 and RWKV_WIDTH % MXU_TILE == 0
assert GDN_HEADS * CHUNK == MXU_TILE
EPS_RMS = 1e-6
EPS_GN = 64e-5
EPS_L2 = 1e-6
MACARON_WEIGHT = 0.5
VMEM_LIMIT = 52 * 1024 * 1024

NT = (((1,), (1,)), ((), ()))
TN = (((0,), (0,)), ((), ()))


def _sigmoid(x):
    return 1.0 / (1.0 + jnp.exp(-x))


def _softplus(x):
    return jnp.maximum(x, 0.0) + jnp.log(1.0 + jnp.exp(-jnp.abs(x)))


def _silu(x):
    return x * _sigmoid(x)


def _dot(a, b, precision=None):
    return jnp.dot(a, b, preferred_element_type=F32, precision=precision)


def _dg(a, b, dims, precision=None):
    return lax.dot_general(a, b, dims, preferred_element_type=F32, precision=precision)


def _bdg(a, b, dims):
    return lax.dot_general(a.astype(BF16), b.astype(BF16), dims, preferred_element_type=F32)


def _split2(a):
    hi = a.astype(BF16)
    return hi, (a - hi.astype(F32)).astype(BF16)


def _stack_bd(xb, mask, reps):
    return jnp.where(mask, jnp.concatenate([xb] * reps, axis=0), jnp.zeros((), BF16))


def _dot3_bd(a, y, stack):
    ah, al = _split2(a)
    yh, yl = _split2(y)
    bh, bl = stack(yh), stack(yl)
    d = lambda p, q: jnp.dot(p, q, preferred_element_type=F32)
    return d(ah, bh) + (d(ah, bl) + d(al, bh))


def _split3(a):
    hi = a.astype(BF16)
    r1 = a - hi.astype(F32)
    mid = r1.astype(BF16)
    return hi, mid, (r1 - mid.astype(F32)).astype(BF16)


def _sel_dot(sel, x):
    sb = sel.astype(BF16)
    hi, mid, lo = _split3(x)
    d = lambda y: jnp.dot(sb, y, preferred_element_type=F32)
    return d(hi) + (d(mid) + d(lo))


def _dot_sel(x, sel):
    sb = sel.astype(BF16)
    hi, mid, lo = _split3(x)
    d = lambda y: jnp.dot(y, sb, preferred_element_type=F32)
    return d(hi) + (d(mid) + d(lo))


def _dot_sel2(x, sel):
    sb = sel.astype(BF16)
    hi, lo = _split2(x)
    return jnp.dot(hi, sb, preferred_element_type=F32) + jnp.dot(lo, sb, preferred_element_type=F32)


def _norm_mod(x, gain, shift, scale):
    ms = jnp.mean(x * x, axis=-1, keepdims=True)
    y = x * lax.rsqrt(ms + EPS_RMS) * gain
    return y * (1.0 + scale) + shift


def _tri(n, strict):
    r = lax.broadcasted_iota(jnp.int32, (n, n), 0)
    c = lax.broadcasted_iota(jnp.int32, (n, n), 1)
    return (r > c) if strict else (r >= c)


def _mod_kernel(c_ref, w_ref, b_ref, o_ref):
    ch, cl = _split2(_silu(c_ref[...]))
    wh, wl = _split2(w_ref[0])
    o_ref[0] = _dot(ch, wh) + (_dot(ch, wl) + _dot(cl, wh)) + b_ref[0]


def _adaln(c, ada_w, ada_b):
    n_l, n_sub, d, d3 = ada_w.shape
    b = c.shape[0]
    rows = -(-b // SUBLANE) * SUBLANE
    c_pad = jnp.pad(c, ((0, rows - b), (0, 0)))
    tn = d
    out = pl.pallas_call(
        _mod_kernel,
        grid=(n_l * n_sub, d3 // tn),
        in_specs=[
            pl.BlockSpec((rows, d), lambda i, j: (0, 0)),
            pl.BlockSpec((1, d, tn), lambda i, j: (i, 0, j)),
            pl.BlockSpec((1, 1, tn), lambda i, j: (i, 0, j)),
        ],
        out_specs=pl.BlockSpec((1, rows, tn), lambda i, j: (i, 0, j)),
        out_shape=jax.ShapeDtypeStruct((n_l * n_sub, rows, d3), F32),
        compiler_params=pltpu.CompilerParams(
            dimension_semantics=("parallel", "parallel"), vmem_limit_bytes=VMEM_LIMIT),
        name="adaln_mod",
    )(c_pad, ada_w.reshape(n_l * n_sub, d, d3), ada_b.reshape(n_l * n_sub, 1, d3))
    return out[:, :b, :]


def _ffn_kernel(*refs, final, mix, sub):
    refs = list(refs)
    x_ref, gain_ref, sh_ref, sc_ref, gt_ref, wgu_ref, wd_ref = refs[:7]
    del refs[:7]
    if mix:
        gm_ref, or_ref, og_ref, w1_ref, w2_ref = refs[:5]
        del refs[:5]
    if final:
        fg_ref = refs.pop(0)
    (o_ref,) = refs

    x = x_ref[...]
    if mix:
        mixed = _dot(or_ref[...].astype(BF16), w1_ref[...]) + _dot(og_ref[...].astype(BF16), w2_ref[...])
        x = x + gm_ref[0] * mixed
    h = _norm_mod(x, gain_ref[...], sh_ref[0], sc_ref[0]).astype(BF16)

    f = wd_ref.shape[0]
    blocks = [(a, min(a + sub, f)) for a in range(0, f, sub)]
    gu = [(_dot(h, wgu_ref[:, a:b]), _dot(h, wgu_ref[:, f + a:f + b])) for a, b in blocks]
    down = None
    for (a, b), (g, u) in zip(blocks, gu):
        part = _dot((_silu(g) * u).astype(BF16), wd_ref[a:b, :])
        down = part if down is None else down + part

    y = x + (MACARON_WEIGHT * gt_ref[0]) * down
    if final:
        ms = jnp.mean(y * y, axis=-1, keepdims=True)
        y = y * lax.rsqrt(ms + EPS_RMS) * fg_ref[...]
    o_ref[...] = y


def _ffn(x, gain, shift, scale, gate, w_gu, w_dn, seq, final_gain=None, mix=None, tm=512,
         sub=3 * MXU_TILE):
    t, d = x.shape
    tm = min(tm, seq)
    per_b = seq // tm
    final = final_gain is not None
    row = lambda i: (i, 0)
    once = dict(index_map=lambda i: (0, 0), pipeline_mode=pl.Buffered(1))
    modspec = pl.BlockSpec((1, 1, d), lambda i: (i // per_b, 0, 0))
    in_specs = [
        pl.BlockSpec((tm, d), row),
        pl.BlockSpec((1, d), **once),
        modspec, modspec, modspec,
        pl.BlockSpec(w_gu.shape, **once), pl.BlockSpec(w_dn.shape, **once),
    ]
    args = [x, gain.reshape(1, d), shift, scale, gate, w_gu, w_dn]
    if mix is not None:
        gate_mix, o_r, o_g, w1, w2 = mix
        in_specs += [modspec, pl.BlockSpec((tm, o_r.shape[1]), row), pl.BlockSpec((tm, o_g.shape[1]), row),
                     pl.BlockSpec(w1.shape, **once), pl.BlockSpec(w2.shape, **once)]
        args += [gate_mix, o_r, o_g, w1, w2]
    if final:
        in_specs.append(pl.BlockSpec((1, d), **once))
        args.append(final_gain.reshape(1, d))
    return pl.pallas_call(
        functools.partial(_ffn_kernel, final=final, mix=mix is not None, sub=sub),
        grid=(t // tm,),
        in_specs=in_specs,
        out_specs=pl.BlockSpec((tm, d), row),
        out_shape=jax.ShapeDtypeStruct((t, d), F32),
        compiler_params=pltpu.CompilerParams(
            dimension_semantics=("parallel",), vmem_limit_bytes=VMEM_LIMIT),
        name="ffn" + ("_mix" if mix is not None else "") + ("_final" if final else ""),
    )(*args)


def _shift_rows(y, carry_row):
    rolled = pltpu.roll(y, 1, 0)
    first = lax.broadcasted_iota(jnp.int32, y.shape, 0) == 0
    return jnp.where(first, carry_row, rolled)


def _inproj_kernel(*refs, has_vres, per_b, tm):
    refs = list(refs)
    x_ref, gain_ref, sh_ref, sc_ref, mu_ref, cw_ref, wr_ref, wq_ref, wz_ref, wb_ref = refs[:10]
    del refs[:10]
    if has_vres:
        vmu_ref, wv_ref = refs[:2]
        del refs[:2]
    or_ref, oq_ref, oz_ref, ob_ref = refs[:4]
    del refs[:4]
    if has_vres:
        ov_ref = refs.pop(0)
    carry_r, ext = refs[:2]
    carry_v = refs[2] if has_vres else None
    pad = SUBLANE

    @pl.when(pl.program_id(0) % per_b == 0)
    def _():
        carry_r[...] = jnp.zeros_like(carry_r)
        ext[0:pad, :] = jnp.zeros((pad, GDN_QKV), F32)
        if has_vres:
            carry_v[...] = jnp.zeros_like(carry_v)

    h = _norm_mod(x_ref[...], gain_ref[...], sh_ref[0], sc_ref[0]).astype(BF16)
    oz_ref[...] = _dot(h, wz_ref[...])
    ob_ref[...] = _dot(h, wb_ref[...])

    y = _dot(h, wr_ref[...])
    prev = _shift_rows(y, carry_r[0:1, :])
    carry_r[0:1, :] = y[tm - 1:tm, :]
    or_ref[...] = y + (prev - y) * mu_ref[...]
    if has_vres:
        yv = _dot(h, wv_ref[...])
        prev_v = _shift_rows(yv, carry_v[0:1, :])
        carry_v[0:1, :] = yv[tm - 1:tm, :]
        ov_ref[...] = yv + (prev_v - yv) * vmu_ref[...]

    q = _dot(h, wq_ref[...])
    ext[pad:pad + tm, :] = q
    conv = q * cw_ref[CONV_WIDTH - 1:CONV_WIDTH, :]
    for j in range(CONV_WIDTH - 1):
        back = CONV_WIDTH - 1 - j
        conv = conv + ext[pad - back:pad - back + tm, :] * cw_ref[j:j + 1, :]
    ext[0:pad, :] = q[tm - pad:tm, :]
    oq_ref[...] = _silu(conv)


def _inproj(x, gain, shift, scale, mu, conv_w, weights, seq, vres_mu=None, tm=512):
    t, d = x.shape
    tm = min(tm, seq)
    per_b = seq // tm
    has_vres = vres_mu is not None
    row = lambda i: (i, 0)
    once = dict(index_map=lambda i: (0, 0), pipeline_mode=pl.Buffered(1))
    modspec = pl.BlockSpec((1, 1, d), lambda i: (i // per_b, 0, 0))
    small = [gain.reshape(1, d), shift, scale, mu, conv_w]
    in_specs = [pl.BlockSpec((tm, d), row), pl.BlockSpec((1, d), **once), modspec, modspec,
                pl.BlockSpec(mu.shape, **once), pl.BlockSpec(conv_w.shape, **once)]
    in_specs += [pl.BlockSpec(w.shape, **once) for w in weights[:4]]
    args = [x] + small + list(weights[:4])
    outs = list(weights[:4])
    scratch = [pltpu.VMEM((SUBLANE, RWKV_COLS), F32), pltpu.VMEM((tm + SUBLANE, GDN_QKV), F32)]
    if has_vres:
        in_specs += [pl.BlockSpec(vres_mu.shape, **once), pl.BlockSpec(weights[4].shape, **once)]
        args += [vres_mu, weights[4]]
        outs.append(weights[4])
        scratch.append(pltpu.VMEM((SUBLANE, LANE), F32))
    return pl.pallas_call(
        functools.partial(_inproj_kernel, has_vres=has_vres, per_b=per_b, tm=tm),
        grid=(t // tm,),
        in_specs=in_specs,
        out_specs=[pl.BlockSpec((tm, w.shape[1]), row) for w in outs],
        out_shape=[jax.ShapeDtypeStruct((t, w.shape[1]), F32) for w in outs],
        scratch_shapes=scratch,
        compiler_params=pltpu.CompilerParams(
            dimension_semantics=("arbitrary",), vmem_limit_bytes=VMEM_LIMIT),
        name="inproj",
    )(*args)


def _rwkv_body(refs, has_vres, tc):
    if has_vres:
        (y_ref, yv_ref, vf_ref, w0_ref, wup_ref, a0_ref, aup_ref, gup_ref, kk_ref, ka_ref,
         rk_ref, gnw_ref, gnb_ref, vup_ref, v0_ref,
         o_ref, state, o_scr) = refs
    else:
        (y_ref, w0_ref, wup_ref, a0_ref, aup_ref, gup_ref, kk_ref, ka_ref,
         rk_ref, gnw_ref, gnb_ref,
         o_ref, v_ref, state, o_scr) = refs
    n, w = RWKV_HEAD_DIM, RWKV_WIDTH

    @pl.when(pl.program_id(1) == 0)
    def _():
        state[...] = jnp.zeros_like(state)

    yield
    ym = y_ref[...]
    r = ym[:, 0:w]
    k = ym[:, w:2 * w]
    v = ym[:, 2 * w:3 * w]
    wd = ym[:, 3 * w:3 * w + DECAY_RANK]
    ad = ym[:, 3 * w + DECAY_RANK:3 * w + DECAY_RANK + ICLR_RANK]
    gd = ym[:, 3 * w + DECAY_RANK + ICLR_RANK:]

    if has_vres:
        logit = v0_ref[...] + _dot(yv_ref[...].astype(BF16), vup_ref[...].astype(BF16))
        v = v + (vf_ref[...] - v) * _sigmoid(logit)
    else:
        v_ref[...] = v

    z = w0_ref[...] + _dot(jnp.tanh(wd).astype(BF16), wup_ref[...].astype(BF16))
    logw = -math.exp(-0.5) * _sigmoid(z)
    a = _sigmoid(a0_ref[...] + _dot(ad.astype(BF16), aup_ref[...].astype(BF16)))
    g = _dot(_sigmoid(gd).astype(BF16), gup_ref[...].astype(BF16))

    gw = MXU_TILE
    hpg = gw // n
    groups = [slice(gi * gw, (gi + 1) * gw) for gi in range(w // gw)]
    rr = lax.broadcasted_iota(jnp.int32, (gw, gw), 0)
    cc = lax.broadcasted_iota(jnp.int32, (gw, gw), 1)
    same_blk = (rr // CHUNK) == (cc // CHUNK)
    head_ones = (rr // n) == (cc // n)
    tt = lax.broadcasted_iota(jnp.int32, (CHUNK, gw), 0)
    ss = lax.broadcasted_iota(jnp.int32, (CHUNK, gw), 1) % CHUNK
    ls_strict = tt > ss
    ls_incl = tt >= ss
    eye_ls = (tt == ss).astype(F32)
    l_incl = _tri(CHUNK, False)

    def head_sum(x):
        return jnp.concatenate([_dot_sel2(x[:, ln], head_ones) for ln in groups], axis=1)

    def bd(x):
        return _stack_bd(x.astype(BF16), same_blk, hpg)

    def ls_dot(a, b_bd):
        return jnp.dot(a.astype(BF16), b_bd, preferred_element_type=F32)

    yield
    kkr = k * kk_ref[...]
    kk = kkr * lax.rsqrt(head_sum(kkr * kkr) + EPS_L2)
    k2 = k * (1.0 + (a - 1.0) * ka_ref[...])
    b = kk * a
    av = -kk

    n_chunks = tc // CHUNK
    insts = [(c, gi) for c in range(n_chunks) for gi in range(len(groups))]
    d = {}
    for c in range(n_chunks):
        rows = slice(c * CHUNK, (c + 1) * CHUNK)
        lw = logw[rows]
        cum = _sel_dot(l_incl, lw)
        last = cum[CHUNK - 1:CHUNK, :]
        e_pos = jnp.exp(cum)
        e_neg = jnp.exp(-cum)
        e_to_end = jnp.exp(last - cum)
        gam_end = jnp.exp(last)
        at = av[rows] * jnp.exp(cum - lw)
        rt = r[rows] * e_pos
        bt = b[rows] * e_neg
        kt = k2[rows] * e_neg
        bg = b[rows] * e_to_end
        kg = k2[rows] * e_to_end
        vc = v[rows]
        for gi, ln in enumerate(groups):
            d[c, gi] = dict(at=at[:, ln], rt=rt[:, ln], bt=bt[:, ln], kt=kt[:, ln], bg=bg[:, ln],
                            kg=kg[:, ln], v=vc[:, ln], gam=gam_end[:, ln])
    yield
    for i in insts:
        e = d[i]
        x_ar = jnp.concatenate([e['at'], e['rt']], axis=0).astype(BF16)
        y_bk = jnp.concatenate([bd(e['bt']), bd(e['kt'])], axis=0)
        m1 = _dg(x_ar, y_bk, NT)
        e['n'] = jnp.where(ls_strict, m1[:CHUNK, :gw], 0.0)
        e['a_ak'] = jnp.where(ls_strict, m1[:CHUNK, gw:], 0.0)
        e['a_rb'] = jnp.where(ls_incl, m1[CHUNK:, :gw], 0.0)
        e['a_rk'] = jnp.where(ls_incl, m1[CHUNK:, gw:], 0.0)
        e['t'] = eye_ls + e['n']
        e['v_bd'] = bd(e['v'])
    yield
    for i in insts:
        d[i]['n'] = ls_dot(d[i]['n'], bd(d[i]['n']))
    yield
    steps = CHUNK.bit_length() - 2
    for k in range(steps):
        for i in insts:
            e = d[i]
            if k + 1 < steps:
                both = ls_dot(jnp.concatenate([e['t'], e['n']], axis=0), bd(e['n']))
                e['t'] = e['t'] + both[:CHUNK]
                e['n'] = both[CHUNK:]
            else:
                e['t'] = e['t'] + ls_dot(e['t'], bd(e['n']))
        yield
    for i in insts:
        e = d[i]
        e['w'] = ls_dot(e['t'], bd(e['at']))
        both = ls_dot(jnp.concatenate([e['a_ak'], e['a_rk']], axis=0), e['v_bd'])
        e['akv'] = both[:CHUNK]
        e['arkv'] = both[CHUNK:]
    yield
    for i in insts:
        e = d[i]
        e['u0'] = ls_dot(e['t'], bd(e['akv']))
        e['q'] = e['rt'] + ls_dot(e['a_rb'], bd(e['w']))
    yield
    for i in insts:
        e = d[i]
        e['o0'] = ls_dot(e['a_rb'], bd(e['u0'])) + e['arkv']
        e['m'] = jnp.where(same_blk, _bdg(e['w'], e['bg'], TN), 0.0)
        uv = jnp.concatenate([e['u0'], e['v']], axis=0)
        bk_end = jnp.concatenate([e['bg'], e['kg']], axis=0)
        e['n0'] = jnp.where(same_blk, _bdg(uv, bk_end, TN), 0.0)
    yield
    for c, gi in insts:
        e = d[c, gi]
        s0 = state[gi]
        sb = s0.astype(BF16)
        o_scr[c * CHUNK:(c + 1) * CHUNK, groups[gi]] = _dg(e['q'].astype(BF16), sb, NT) + e['o0']
        state[gi] = s0 * e['gam'] + jnp.dot(sb, e['m'].astype(BF16), preferred_element_type=F32) + e['n0']
    yield

    o = o_scr[...]
    inv_n = 1.0 / n
    mean = head_sum(o) * inv_n
    dev = o - mean
    var = head_sum(dev * dev) * inv_n
    on = dev * lax.rsqrt(var + EPS_GN) * gnw_ref[...] + gnb_ref[...]
    bonus = head_sum(r * k2 * rk_ref[...])
    o_ref[...] = (on + bonus * v) * g


def _gdn_body(refs, tc):
    (qkv_ref, z_ref, ba_ref, alog_r_ref, dt_r_ref, alog_c_ref, dt_c_ref, nw_ref,
     o_ref, state, o_scr) = refs
    dh, nh, w = GDN_HEAD_DIM, GDN_HEADS, GDN_WIDTH
    pad = SUBLANE

    @pl.when(pl.program_id(1) == 0)
    def _():
        state[...] = jnp.zeros_like(state)

    yield
    qkv = qkv_ref[...]

    ba = ba_ref[...]
    beta_c = _sigmoid(ba)
    g_c = -jnp.exp(alog_r_ref[...]) * _softplus(ba + dt_r_ref[...])
    er = lax.broadcasted_iota(jnp.int32, (pad, LANE), 0)
    ec = lax.broadcasted_iota(jnp.int32, (pad, LANE), 1)
    pick = (er == ec).astype(BF16)
    ba_hi, ba_mid, ba_lo = _split3(ba)
    ba_t = _dg(pick, ba_hi, NT) + (_dg(pick, ba_mid, NT) + _dg(pick, ba_lo, NT))
    g_r = -jnp.exp(alog_c_ref[...]) * _softplus(ba_t + dt_c_ref[...])

    yield
    gw = nh * CHUNK
    rr = lax.broadcasted_iota(jnp.int32, (gw, gw), 0)
    cc = lax.broadcasted_iota(jnp.int32, (gw, gw), 1)
    same_blk = (rr // CHUNK) == (cc // CHUNK)
    u_bd = jnp.logical_and(same_blk, rr <= cc)
    tt = lax.broadcasted_iota(jnp.int32, (CHUNK, gw), 0)
    ss = lax.broadcasted_iota(jnp.int32, (CHUNK, gw), 1) % CHUNK
    ls_strict = tt > ss
    ls_incl = tt >= ss
    eye_ls = (tt == ss).astype(F32)
    l_incl = _tri(CHUNK, False)
    src = lax.broadcasted_iota(jnp.int32, (LANE, gw), 0)
    spread_g = src == nh + lax.broadcasted_iota(jnp.int32, (LANE, gw), 1) // CHUNK
    spread_b = src == lax.broadcasted_iota(jnp.int32, (LANE, gw), 1) // CHUNK
    srcw = lax.broadcasted_iota(jnp.int32, (LANE, w), 0)
    spread_gw = srcw == nh + lax.broadcasted_iota(jnp.int32, (LANE, w), 1) // dh
    spread_bw = srcw == lax.broadcasted_iota(jnp.int32, (LANE, w), 1) // dh
    head_rows = (lax.broadcasted_iota(jnp.int32, (gw, w), 0) // CHUNK
                 == lax.broadcasted_iota(jnp.int32, (gw, w), 1) // dh)

    def bd(xb):
        return _stack_bd(xb, same_blk, nh)

    def bd_wide(xb):
        return _stack_bd(xb, head_rows, nh)

    def ls_dot(a, y):
        return jnp.dot(a.astype(BF16), bd(y.astype(BF16)), preferred_element_type=F32)

    def l2norm_heads(x):
        parts = []
        for h in range(nh):
            xh = x[:, h * dh:(h + 1) * dh]
            parts.append(xh * lax.rsqrt(jnp.sum(xh * xh, axis=-1, keepdims=True) + EPS_L2))
        return jnp.concatenate(parts, axis=1)

    n_chunks = tc // CHUNK
    heads = [slice(h * dh, (h + 1) * dh) for h in range(nh)]
    d = []
    for c in range(n_chunks):
        rows = slice(c * CHUNK, (c + 1) * CHUNK)
        g_rows = g_c[rows]
        b_rows = beta_c[rows]
        cum_c = _sel_dot(l_incl, g_rows)
        acum = _dot_sel(cum_c, spread_g)
        cumw = _dot_sel(cum_c, spread_gw)
        beta = _dot_sel(b_rows, spread_b)
        betaw = _dot_sel(b_rows, spread_bw)
        g_row = jnp.concatenate([g_r[nh + h:nh + h + 1, rows] for h in range(nh)], axis=1)
        crow = _dot_sel(jnp.broadcast_to(g_row, (SUBLANE, gw)), u_bd)[0:1, :]
        decay = jnp.where(ls_incl, jnp.exp(jnp.where(ls_incl, acum - crow, 0.0)), 0.0)
        qc = l2norm_heads(qkv[rows, 0:w]) * (dh ** -0.5)
        kc = l2norm_heads(qkv[rows, w:2 * w])
        vc = qkv[rows, 2 * w:3 * w]
        kq = jnp.concatenate([kc, qc], axis=0).astype(BF16)
        m = _dg(kq, bd_wide(kc.astype(BF16)), NT)
        n = -jnp.where(ls_strict, m[:CHUNK] * decay * beta, 0.0)
        d.append(dict(n1=n, n=n, t=eye_ls + n, p=m[CHUNK:] * decay, cumw=cumw, betaw=betaw, qc=qc, kc=kc, vc=vc))
    yield
    for e in d:
        e['n'] = ls_dot(e['n'], e['n'])
    yield
    steps = CHUNK.bit_length() - 2
    for k in range(steps):
        for e in d:
            if k + 1 < steps:
                both = ls_dot(jnp.concatenate([e['t'], e['n']], axis=0), e['n'])
                e['t'] = e['t'] + both[:CHUNK]
                e['n'] = both[CHUNK:]
            else:
                e['t'] = e['t'] + ls_dot(e['t'], e['n'])
        yield
    for e in d:
        e['r'] = eye_ls - e['t'] + _dot3_bd(e['n1'], e['t'], bd)
    yield
    for e in d:
        e['t'] = e['t'] + ls_dot(e['t'], e['r'])
    yield
    for e in d:
        e_c = jnp.exp(e['cumw'])
        clast = e['cumw'][CHUNK - 1:CHUNK, :]
        tb = e['t'].astype(BF16)
        e['u'] = _dot(tb, bd_wide((e['vc'] * e['betaw']).astype(BF16)))
        e['wm'] = _dot(tb, bd_wide((e['kc'] * (e['betaw'] * e_c)).astype(BF16)))
        e['q_dec'] = e['qc'] * e_c
        e['k_dec'] = e['kc'] * jnp.exp(clast - e['cumw'])
        e['g'] = jnp.exp(clast)
    yield
    for e in d:
        pb = e['p'].astype(BF16)
        e['qe'] = e['q_dec'] - _dot(pb, bd_wide(e['wm'].astype(BF16)))
        e['o0'] = _dot(pb, bd_wide(e['u'].astype(BF16)))
        e['kw'] = [_bdg(e['k_dec'][:, ln], e['wm'][:, ln], TN) for ln in heads]
        e['n0'] = [_bdg(e['k_dec'][:, ln], e['u'][:, ln], TN) for ln in heads]
    yield
    for c, e in enumerate(d):
        outs = []
        for h, ln in enumerate(heads):
            s0 = state[h]
            sb = s0.astype(BF16)
            outs.append(_dot(e['qe'][:, ln].astype(BF16), sb) + e['o0'][:, ln])
            state[h] = s0 * e['g'][:, ln] - _dot(e['kw'][h].astype(BF16), sb) + e['n0'][h]
        o_scr[c * CHUNK:(c + 1) * CHUNK, :] = jnp.concatenate(outs, axis=1)
    yield

    zs = _silu(z_ref[...])
    for h in range(nh):
        ln = slice(h * dh, (h + 1) * dh)
        o = o_scr[:, ln]
        o = o * lax.rsqrt(jnp.mean(o * o, axis=-1, keepdims=True) + EPS_RMS) * nw_ref[...]
        o_ref[:, ln] = o * zs[:, ln]


def _mixer_kernel(*refs, n_in_r, n_in_g, n_out_r, has_vres, tc):
    refs = list(refs)
    in_r, refs = refs[:n_in_r], refs[n_in_r:]
    in_g, refs = refs[:n_in_g], refs[n_in_g:]
    out_r, refs = refs[:n_out_r], refs[n_out_r:]
    out_g, refs = refs[:1], refs[1:]
    scr_r, scr_g = refs[:2], refs[2:]
    bodies = [_rwkv_body(in_r + out_r + scr_r, has_vres, tc), _gdn_body(in_g + out_g + scr_g, tc)]
    while bodies:
        for body in list(bodies):
            if next(body, bodies) is bodies:
                bodies.remove(body)


def _mixer(y_r, rparams, y_qkv, y_z, y_ba, gparams, seq, y_vres=None, v_first=None, vres_params=None, tc=256):
    t = y_r.shape[0]
    tc = min(tc, seq)
    per_b = seq // tc
    nb = t // seq
    has_vres = y_vres is not None
    w = RWKV_WIDTH
    row = lambda b, s: (b * per_b + s, 0)
    once = dict(index_map=lambda b, s: (0, 0), pipeline_mode=pl.Buffered(1))
    tile = lambda width: pl.BlockSpec((tc, width), row)

    in_r = [y_r] + ([y_vres, v_first] if has_vres else [])
    spec_r = [tile(a.shape[1]) for a in in_r]
    plist = list(rparams) + (list(vres_params) if has_vres else [])
    in_r += plist
    spec_r += [pl.BlockSpec(p.shape, **once) for p in plist]
    in_g = [y_qkv, y_z, y_ba] + list(gparams)
    spec_g = [tile(GDN_QKV), tile(GDN_WIDTH), tile(LANE)] + [pl.BlockSpec(p.shape, **once) for p in gparams]

    n_out_r = 1 if has_vres else 2
    n_out = n_out_r + 1
    scratch = [pltpu.VMEM((RWKV_WIDTH // MXU_TILE, MXU_TILE, MXU_TILE), F32), pltpu.VMEM((tc, w), F32),
               pltpu.VMEM((GDN_HEADS, GDN_HEAD_DIM, GDN_HEAD_DIM), F32), pltpu.VMEM((tc, GDN_WIDTH), F32)]
    return pl.pallas_call(
        functools.partial(_mixer_kernel, n_in_r=len(in_r), n_in_g=len(in_g), n_out_r=n_out_r,
                          has_vres=has_vres, tc=tc),
        grid=(nb, per_b),
        in_specs=spec_r + spec_g,
        out_specs=[tile(w)] * n_out,
        out_shape=[jax.ShapeDtypeStruct((t, w), F32)] * n_out,
        scratch_shapes=scratch,
        compiler_params=pltpu.CompilerParams(
            dimension_semantics=("parallel", "arbitrary"), vmem_limit_bytes=VMEM_LIMIT),
        name="mixer_vres" if has_vres else "mixer",
    )(*in_r, *in_g)


def _pad_cols(m, width):
    return jnp.pad(m, ((0, 0), (0, width - m.shape[1])))


def kernel(x, c, norm_gain, ada_w, ada_b, ffn_w_gu, ffn_w_down, w_in, w_out, rwkv_mu, rwkv_w0, rwkv_w_up, rwkv_a0, rwkv_a_up, rwkv_g_up, rwkv_k_k, rwkv_k_a, rwkv_r_k, rwkv_gn_w, rwkv_gn_b, vres_w_down, vres_mu, vres_w_up, vres_v0, gdn_conv_w, gdn_a_log, gdn_dt_bias, gdn_norm_w, final_gain):
    bsz, seq, d = x.shape
    depth = norm_gain.shape[0]
    nh = GDN_HEADS
    mod = _adaln(c, ada_w, ada_b)
    xf = x.reshape(bsz * seq, d)
    v_first = None
    row = lambda p: p.reshape(1, -1)
    for l in range(depth):
        def mods(sub):
            m = mod[l * 3 + sub]
            return tuple(m[:, i * d:(i + 1) * d].reshape(bsz, 1, d) for i in range(3))

        shift, scale, gate = mods(0)
        xf = _ffn(xf, norm_gain[l, 0], shift, scale, gate,
                  ffn_w_gu[l, 0].astype(BF16), ffn_w_down[l, 0].astype(BF16), seq)

        shift, scale, gate = mods(1)
        wl = w_in[l]
        w_ba = _pad_cols(wl[:, RWKV_COLS + 4 * GDN_WIDTH:], LANE)
        weights = [wl[:, :RWKV_COLS], wl[:, RWKV_COLS:RWKV_COLS + GDN_QKV],
                   wl[:, RWKV_COLS + GDN_QKV:RWKV_COLS + 4 * GDN_WIDTH], w_ba]
        if l > 0:
            weights.append(_pad_cols(vres_w_down[l - 1], LANE))
        ys = _inproj(xf, norm_gain[l, 1], shift, scale, row(rwkv_mu[l]), gdn_conv_w[l],
                     [wt.astype(BF16) for wt in weights], seq,
                     vres_mu=_pad_cols(row(vres_mu[l - 1]), LANE) if l > 0 else None)
        y_r, y_qkv, y_z, y_ba = ys[:4]

        rparams = [row(rwkv_w0[l]), rwkv_w_up[l], row(rwkv_a0[l]), rwkv_a_up[l],
                   rwkv_g_up[l], row(rwkv_k_k[l]), row(rwkv_k_a[l]), row(rwkv_r_k[l]),
                   row(rwkv_gn_w[l]), row(rwkv_gn_b[l])]
        alog = gdn_a_log[l].astype(F32)
        dtb = gdn_dt_bias[l].astype(F32)
        place = lambda p: jnp.pad(p, (nh, LANE - 2 * nh))
        gparams = [row(place(alog)), row(place(dtb)),
                   place(alog)[:SUBLANE].reshape(SUBLANE, 1), place(dtb)[:SUBLANE].reshape(SUBLANE, 1),
                   row(gdn_norm_w[l])]
        if l == 0:
            o_r, v_first, o_g = _mixer(y_r, rparams, y_qkv, y_z, y_ba, gparams, seq)
        else:
            vparams = [jnp.pad(vres_w_up[l - 1], ((0, LANE - VRES_RANK), (0, 0))),
                       row(vres_v0[l - 1])]
            o_r, o_g = _mixer(y_r, rparams, y_qkv, y_z, y_ba, gparams, seq,
                              y_vres=ys[4], v_first=v_first, vres_params=vparams)

        wo = w_out[l].astype(BF16)
        mix = (gate, o_r, o_g, wo[:RWKV_WIDTH], wo[RWKV_WIDTH:])
        shift, scale, gate = mods(2)
        xf = _ffn(xf, norm_gain[l, 2], shift, scale, gate,
                  ffn_w_gu[l, 1].astype(BF16), ffn_w_down[l, 1].astype(BF16), seq,
                  final_gain=final_gain if l == depth - 1 else None, mix=mix)
    return xf.reshape(bsz, seq, d)
```

```python
import functools
import math

import jax
import jax.numpy as jnp
from jax import lax
from jax.experimental import pallas as pl
from jax.experimental.pallas import tpu as pltpu

F32 = jnp.float32
BF16 = jnp.bfloat16

RWKV_HEAD_DIM = 64
RWKV_HEADS = 8
RWKV_WIDTH = RWKV_HEAD_DIM * RWKV_HEADS
GDN_HEAD_DIM = 128
GDN_HEADS = 4
GDN_WIDTH = GDN_HEAD_DIM * GDN_HEADS
DECAY_RANK = 64
ICLR_RANK = 64
VRES_RANK = 32
GATE_RANK = 128
RWKV_COLS = 3 * RWKV_WIDTH + DECAY_RANK + ICLR_RANK + GATE_RANK
GDN_QKV = 3 * GDN_WIDTH
CONV_WIDTH = 4
CHUNK = 64
LANE = 128
SUBLANE = 8
MXU_TILE = 256
assert CHUNK == RWKV_HEAD_DIM and MXU_TILE % RWKV_HEAD_DIM == 0 and RWKV_WIDTH % MXU_TILE == 0
assert GDN_HEADS * CHUNK == MXU_TILE
EPS_RMS = 1e-6
EPS_GN = 64e-5
EPS_L2 = 1e-6
MACARON_WEIGHT = 0.5
VMEM_LIMIT = 52 * 1024 * 1024

NT = (((1,), (1,)), ((), ()))
TN = (((0,), (0,)), ((), ()))


def _sigmoid(x):
    return 1.0 / (1.0 + jnp.exp(-x))


def _softplus(x):
    return jnp.maximum(x, 0.0) + jnp.log(1.0 + jnp.exp(-jnp.abs(x)))


def _silu(x):
    return x * _sigmoid(x)


def _dot(a, b, precision=None):
    return jnp.dot(a, b, preferred_element_type=F32, precision=precision)


def _dg(a, b, dims, precision=None):
    return lax.dot_general(a, b, dims, preferred_element_type=F32, precision=precision)


def _bdg(a, b, dims):
    return lax.dot_general(a.astype(BF16), b.astype(BF16), dims, preferred_element_type=F32)


def _split2(a):
    hi = a.astype(BF16)
    return hi, (a - hi.astype(F32)).astype(BF16)


def _stack_bd(xb, mask, reps):
    return jnp.where(mask, jnp.concatenate([xb] * reps, axis=0), jnp.zeros((), BF16))


def _dot3_bd(a, y, stack):
    ah, al = _split2(a)
    yh, yl = _split2(y)
    bh, bl = stack(yh), stack(yl)
    d = lambda p, q: jnp.dot(p, q, preferred_element_type=F32)
    rows = a.shape[0]
    both = d(jnp.concatenate([ah, al], axis=0), bh)
    return both[:rows] + (both[rows:] + d(ah, bl))


def _split3(a):
    hi = a.astype(BF16)
    r1 = a - hi.astype(F32)
    mid = r1.astype(BF16)
    return hi, mid, (r1 - mid.astype(F32)).astype(BF16)


def _sel_dot(sel, x):
    sb = sel.astype(BF16)
    hi, mid, lo = _split3(x)
    d = lambda y: jnp.dot(sb, y, preferred_element_type=F32)
    return d(hi) + (d(mid) + d(lo))


def _dot_sel(x, sel):
    sb = sel.astype(BF16)
    hi, mid, lo = _split3(x)
    d = lambda y: jnp.dot(y, sb, preferred_element_type=F32)
    return d(hi) + (d(mid) + d(lo))


def _norm_mod(x, gain, shift, scale):
    ms = jnp.mean(x * x, axis=-1, keepdims=True)
    y = x * lax.rsqrt(ms + EPS_RMS) * gain
    return y * (1.0 + scale) + shift


def _tri(n, strict):
    r = lax.broadcasted_iota(jnp.int32, (n, n), 0)
    c = lax.broadcasted_iota(jnp.int32, (n, n), 1)
    return (r > c) if strict else (r >= c)


def _mod_kernel(c_ref, w_ref, b_ref, o_ref):
    ch, cl = _split2(_silu(c_ref[...]))
    wh, wl = _split2(w_ref[0])
    o_ref[0] = _dot(ch, wh) + (_dot(ch, wl) + _dot(cl, wh)) + b_ref[0]


def _adaln(c, ada_w, ada_b):
    n_l, n_sub, d, d3 = ada_w.shape
    b = c.shape[0]
    rows = -(-b // SUBLANE) * SUBLANE
    c_pad = jnp.pad(c, ((0, rows - b), (0, 0)))
    tn = d3
    out = pl.pallas_call(
        _mod_kernel,
        grid=(n_l * n_sub, d3 // tn),
        in_specs=[
            pl.BlockSpec((rows, d), lambda i, j: (0, 0)),
            pl.BlockSpec((1, d, tn), lambda i, j: (i, 0, j)),
            pl.BlockSpec((1, 1, tn), lambda i, j: (i, 0, j)),
        ],
        out_specs=pl.BlockSpec((1, rows, tn), lambda i, j: (i, 0, j)),
        out_shape=jax.ShapeDtypeStruct((n_l * n_sub, rows, d3), F32),
        compiler_params=pltpu.CompilerParams(
            dimension_semantics=("parallel", "parallel"), vmem_limit_bytes=VMEM_LIMIT),
        name="adaln_mod",
    )(c_pad, ada_w.reshape(n_l * n_sub, d, d3), ada_b.reshape(n_l * n_sub, 1, d3))
    return out[:, :b, :]


def _ffn_kernel(*refs, final, mix, sub):
    refs = list(refs)
    x_ref, gain_ref, sh_ref, sc_ref, gt_ref, wgu_ref, wd_ref = refs[:7]
    del refs[:7]
    if mix:
        gm_ref, or_ref, og_ref, w1_ref, w2_ref = refs[:5]
        del refs[:5]
    if final:
        fg_ref = refs.pop(0)
    (o_ref,) = refs

    x = x_ref[...]
    if mix:
        mixed = _dot(or_ref[...].astype(BF16), w1_ref[...]) + _dot(og_ref[...].astype(BF16), w2_ref[...])
        x = x + gm_ref[0] * mixed
    h = _norm_mod(x, gain_ref[...], sh_ref[0], sc_ref[0]).astype(BF16)

    f = wd_ref.shape[0]
    blocks = [(a, min(a + sub, f)) for a in range(0, f, sub)]
    gu = [(_dot(h, wgu_ref[:, a:b]), _dot(h, wgu_ref[:, f + a:f + b])) for a, b in blocks]
    down = None
    for (a, b), (g, u) in zip(blocks, gu):
        part = _dot((_silu(g) * u).astype(BF16), wd_ref[a:b, :])
        down = part if down is None else down + part

    y = x + (MACARON_WEIGHT * gt_ref[0]) * down
    if final:
        ms = jnp.mean(y * y, axis=-1, keepdims=True)
        y = y * lax.rsqrt(ms + EPS_RMS) * fg_ref[...]
    o_ref[...] = y


def _ffn(x, gain, shift, scale, gate, w_gu, w_dn, seq, final_gain=None, mix=None, tm=512,
         sub=3 * MXU_TILE):
    t, d = x.shape
    tm = min(tm, seq)
    per_b = seq // tm
    final = final_gain is not None
    row = lambda i: (i, 0)
    once = dict(index_map=lambda i: (0, 0), pipeline_mode=pl.Buffered(1))
    modspec = pl.BlockSpec((1, 1, d), lambda i: (i // per_b, 0, 0))
    in_specs = [
        pl.BlockSpec((tm, d), row),
        pl.BlockSpec((1, d), **once),
        modspec, modspec, modspec,
        pl.BlockSpec(w_gu.shape, **once), pl.BlockSpec(w_dn.shape, **once),
    ]
    args = [x, gain.reshape(1, d), shift, scale, gate, w_gu, w_dn]
    if mix is not None:
        gate_mix, o_r, o_g, w1, w2 = mix
        in_specs += [modspec, pl.BlockSpec((tm, o_r.shape[1]), row), pl.BlockSpec((tm, o_g.shape[1]), row),
                     pl.BlockSpec(w1.shape, **once), pl.BlockSpec(w2.shape, **once)]
        args += [gate_mix, o_r, o_g, w1, w2]
    if final:
        in_specs.append(pl.BlockSpec((1, d), **once))
        args.append(final_gain.reshape(1, d))
    return pl.pallas_call(
        functools.partial(_ffn_kernel, final=final, mix=mix is not None, sub=sub),
        grid=(t // tm,),
        in_specs=in_specs,
        out_specs=pl.BlockSpec((tm, d), row),
        out_shape=jax.ShapeDtypeStruct((t, d), F32),
        compiler_params=pltpu.CompilerParams(
            dimension_semantics=("parallel",), vmem_limit_bytes=VMEM_LIMIT),
        name="ffn" + ("_mix" if mix is not None else "") + ("_final" if final else ""),
    )(*args)


def _shift_rows(y, carry_row):
    rolled = pltpu.roll(y, 1, 0)
    first = lax.broadcasted_iota(jnp.int32, y.shape, 0) == 0
    return jnp.where(first, carry_row, rolled)


def _inproj_kernel(*refs, has_vres, per_b, tm, col):
    refs = list(refs)
    x_ref, gain_ref, sh_ref, sc_ref, mu_ref, cw_ref, wr_ref, wq_ref, wz_ref, wb_ref = refs[:10]
    del refs[:10]
    if has_vres:
        vmu_ref, wv_ref = refs[:2]
        del refs[:2]
    or_ref, oq_ref, oz_ref, ob_ref = refs[:4]
    del refs[:4]
    if has_vres:
        ov_ref = refs.pop(0)
    carry_r, ext = refs[:2]
    carry_v = refs[2] if has_vres else None
    pad = SUBLANE

    @pl.when(pl.program_id(0) % per_b == 0)
    def _():
        carry_r[...] = jnp.zeros_like(carry_r)
        ext[0:pad, :] = jnp.zeros((pad, GDN_QKV), F32)
        if has_vres:
            carry_v[...] = jnp.zeros_like(carry_v)

    h = _norm_mod(x_ref[...], gain_ref[...], sh_ref[0], sc_ref[0]).astype(BF16)
    blocks = lambda width: [(a, min(a + col, width)) for a in range(0, width, col)]
    for a, b in blocks(GDN_QKV):
        q = _dot(h, wq_ref[:, a:b])
        ext[pad:pad + tm, a:b] = q
        conv = q * cw_ref[CONV_WIDTH - 1:CONV_WIDTH, a:b]
        for j in range(CONV_WIDTH - 1):
            back = CONV_WIDTH - 1 - j
            conv = conv + ext[pad - back:pad - back + tm, a:b] * cw_ref[j:j + 1, a:b]
        ext[0:pad, a:b] = q[tm - pad:tm, :]
        oq_ref[:, a:b] = _silu(conv)

    for a, b in blocks(RWKV_COLS):
        y = _dot(h, wr_ref[:, a:b])
        prev = _shift_rows(y, carry_r[0:1, a:b])
        carry_r[0:1, a:b] = y[tm - 1:tm, :]
        or_ref[:, a:b] = y + (prev - y) * mu_ref[:, a:b]
    if has_vres:
        yv = _dot(h, wv_ref[...])
        prev_v = _shift_rows(yv, carry_v[0:1, :])
        carry_v[0:1, :] = yv[tm - 1:tm, :]
        ov_ref[...] = yv + (prev_v - yv) * vmu_ref[...]

    oz_ref[...] = _dot(h, wz_ref[...])
    ob_ref[...] = _dot(h, wb_ref[...])


def _inproj(x, gain, shift, scale, mu, conv_w, weights, seq, vres_mu=None, tm=512, col=2 * MXU_TILE):
    t, d = x.shape
    tm = min(tm, seq)
    per_b = seq // tm
    has_vres = vres_mu is not None
    row = lambda i: (i, 0)
    once = dict(index_map=lambda i: (0, 0), pipeline_mode=pl.Buffered(1))
    modspec = pl.BlockSpec((1, 1, d), lambda i: (i // per_b, 0, 0))
    small = [gain.reshape(1, d), shift, scale, mu, conv_w]
    in_specs = [pl.BlockSpec((tm, d), row), pl.BlockSpec((1, d), **once), modspec, modspec,
                pl.BlockSpec(mu.shape, **once), pl.BlockSpec(conv_w.shape, **once)]
    in_specs += [pl.BlockSpec(w.shape, **once) for w in weights[:4]]
    args = [x] + small + list(weights[:4])
    outs = list(weights[:4])
    scratch = [pltpu.VMEM((SUBLANE, RWKV_COLS), F32), pltpu.VMEM((tm + SUBLANE, GDN_QKV), F32)]
    if has_vres:
        in_specs += [pl.BlockSpec(vres_mu.shape, **once), pl.BlockSpec(weights[4].shape, **once)]
        args += [vres_mu, weights[4]]
        outs.append(weights[4])
        scratch.append(pltpu.VMEM((SUBLANE, LANE), F32))
    return pl.pallas_call(
        functools.partial(_inproj_kernel, has_vres=has_vres, per_b=per_b, tm=tm, col=col),
        grid=(t // tm,),
        in_specs=in_specs,
        out_specs=[pl.BlockSpec((tm, w.shape[1]), row) for w in outs],
        out_shape=[jax.ShapeDtypeStruct((t, w.shape[1]), F32) for w in outs],
        scratch_shapes=scratch,
        compiler_params=pltpu.CompilerParams(
            dimension_semantics=("arbitrary",), vmem_limit_bytes=VMEM_LIMIT),
        name="inproj",
    )(*args)


def _rwkv_body(refs, has_vres, tc):
    if has_vres:
        (y_ref, yv_ref, vf_ref, w0_ref, wup_ref, a0_ref, aup_ref, gup_ref, kk_ref, ka_ref,
         rk_ref, gnw_ref, gnb_ref, vup_ref, v0_ref,
         o_ref, state, o_scr) = refs
    else:
        (y_ref, w0_ref, wup_ref, a0_ref, aup_ref, gup_ref, kk_ref, ka_ref,
         rk_ref, gnw_ref, gnb_ref,
         o_ref, v_ref, state, o_scr) = refs
    n, w = RWKV_HEAD_DIM, RWKV_WIDTH

    @pl.when(pl.program_id(1) == 0)
    def _():
        state[...] = jnp.zeros_like(state)

    yield
    ym = y_ref[...]
    r = ym[:, 0:w]
    k = ym[:, w:2 * w]
    v = ym[:, 2 * w:3 * w]
    wd = ym[:, 3 * w:3 * w + DECAY_RANK]
    ad = ym[:, 3 * w + DECAY_RANK:3 * w + DECAY_RANK + ICLR_RANK]
    gd = ym[:, 3 * w + DECAY_RANK + ICLR_RANK:]

    if has_vres:
        logit = v0_ref[...] + _dot(yv_ref[...].astype(BF16), vup_ref[...].astype(BF16))
        v = v + (vf_ref[...] - v) * _sigmoid(logit)
    else:
        v_ref[...] = v

    z = w0_ref[...] + _dot(jnp.tanh(wd).astype(BF16), wup_ref[...].astype(BF16))
    logw = -math.exp(-0.5) * _sigmoid(z)
    a = _sigmoid(a0_ref[...] + _dot(ad.astype(BF16), aup_ref[...].astype(BF16)))
    g = _dot(_sigmoid(gd).astype(BF16), gup_ref[...].astype(BF16))

    gw = MXU_TILE
    hpg = gw // n
    groups = [slice(gi * gw, (gi + 1) * gw) for gi in range(w // gw)]
    rr = lax.broadcasted_iota(jnp.int32, (gw, gw), 0)
    cc = lax.broadcasted_iota(jnp.int32, (gw, gw), 1)
    same_blk = (rr // CHUNK) == (cc // CHUNK)
    head_ones = (rr // n) == (cc // n)
    tt = lax.broadcasted_iota(jnp.int32, (CHUNK, gw), 0)
    ss = lax.broadcasted_iota(jnp.int32, (CHUNK, gw), 1) % CHUNK
    ls_strict = tt > ss
    ls_incl = tt >= ss
    eye_ls = (tt == ss).astype(F32)
    l_incl = _tri(CHUNK, False)

    def head_sum(x):
        hi, lo = _split2(x)
        rows = x.shape[0]
        parts = [p[:, ln] for p in (hi, lo) for ln in groups]
        r = jnp.dot(jnp.concatenate(parts, axis=0), head_ones.astype(BF16), preferred_element_type=F32)
        ng = len(groups)
        return jnp.concatenate([r[gi * rows:(gi + 1) * rows] + r[(ng + gi) * rows:(ng + gi + 1) * rows]
                                for gi in range(ng)], axis=1)

    def bd(x):
        return _stack_bd(x.astype(BF16), same_blk, hpg)

    def ls_dot(a, b_bd):
        return jnp.dot(a.astype(BF16), b_bd, preferred_element_type=F32)

    yield
    kkr = k * kk_ref[...]
    kk = kkr * lax.rsqrt(head_sum(kkr * kkr) + EPS_L2)
    k2 = k * (1.0 + (a - 1.0) * ka_ref[...])
    b = kk * a
    av = -kk

    n_chunks = tc // CHUNK
    insts = [(c, gi) for c in range(n_chunks) for gi in range(len(groups))]
    d = {}
    for c in range(n_chunks):
        rows = slice(c * CHUNK, (c + 1) * CHUNK)
        lw = logw[rows]
        cum = _sel_dot(l_incl, lw)
        last = cum[CHUNK - 1:CHUNK, :]
        e_pos = jnp.exp(cum)
        e_neg = jnp.exp(-cum)
        e_to_end = jnp.exp(last - cum)
        gam_end = jnp.exp(last)
        at = av[rows] * jnp.exp(cum - lw)
        rt = r[rows] * e_pos
        bt = b[rows] * e_neg
        kt = k2[rows] * e_neg
        bg = b[rows] * e_to_end
        kg = k2[rows] * e_to_end
        vc = v[rows]
        for gi, ln in enumerate(groups):
            d[c, gi] = dict(at=at[:, ln], rt=rt[:, ln], bt=bt[:, ln], kt=kt[:, ln], bg=bg[:, ln],
                            kg=kg[:, ln], v=vc[:, ln], gam=gam_end[:, ln])
    yield
    for i in insts:
        e = d[i]
        x_ar = jnp.concatenate([e['at'], e['rt']], axis=0).astype(BF16)
        y_bk = jnp.concatenate([bd(e['bt']), bd(e['kt'])], axis=0)
        m1 = _dg(x_ar, y_bk, NT)
        e['n'] = jnp.where(ls_strict, m1[:CHUNK, :gw], 0.0)
        e['a_ak'] = jnp.where(ls_strict, m1[:CHUNK, gw:], 0.0)
        e['a_rb'] = jnp.where(ls_incl, m1[CHUNK:, :gw], 0.0)
        e['a_rk'] = jnp.where(ls_incl, m1[CHUNK:, gw:], 0.0)
        e['t'] = eye_ls + e['n']
        e['v_bd'] = bd(e['v'])
    yield
    for i in insts:
        d[i]['n'] = ls_dot(d[i]['n'], bd(d[i]['n']))
    yield
    steps = CHUNK.bit_length() - 2
    for k in range(steps):
        for i in insts:
            e = d[i]
            if k + 1 < steps:
                both = ls_dot(jnp.concatenate([e['t'], e['n']], axis=0), bd(e['n']))
                e['t'] = e['t'] + both[:CHUNK]
                e['n'] = both[CHUNK:]
            else:
                e['t'] = e['t'] + ls_dot(e['t'], bd(e['n']))
        yield
    for i in insts:
        e = d[i]
        e['w'] = ls_dot(e['t'], bd(e['at']))
        both = ls_dot(jnp.concatenate([e['a_ak'], e['a_rk']], axis=0), e['v_bd'])
        e['akv'] = both[:CHUNK]
        e['arkv'] = both[CHUNK:]
    yield
    for i in insts:
        e = d[i]
        e['u0'] = ls_dot(e['t'], bd(e['akv']))
        e['q'] = e['rt'] + ls_dot(e['a_rb'], bd(e['w']))
    yield
    for i in insts:
        e = d[i]
        e['o0'] = ls_dot(e['a_rb'], bd(e['u0'])) + e['arkv']
        e['m'] = jnp.where(same_blk, _bdg(e['w'], e['bg'], TN), 0.0)
        uv = jnp.concatenate([e['u0'], e['v']], axis=0)
        bk_end = jnp.concatenate([e['bg'], e['kg']], axis=0)
        e['n0'] = jnp.where(same_blk, _bdg(uv, bk_end, TN), 0.0)
    yield
    for c, gi in insts:
        e = d[c, gi]
        s0 = state[gi]
        sb = s0.astype(BF16)
        o_scr[c * CHUNK:(c + 1) * CHUNK, groups[gi]] = _dg(e['q'].astype(BF16), sb, NT) + e['o0']
        state[gi] = s0 * e['gam'] + jnp.dot(sb, e['m'].astype(BF16), preferred_element_type=F32) + e['n0']
    yield

    o = o_scr[...]
    inv_n = 1.0 / n
    mean = head_sum(o) * inv_n
    dev = o - mean
    var = head_sum(dev * dev) * inv_n
    on = dev * lax.rsqrt(var + EPS_GN) * gnw_ref[...] + gnb_ref[...]
    bonus = head_sum(r * k2 * rk_ref[...])
    o_ref[...] = (on + bonus * v) * g


def _gdn_body(refs, tc):
    (qkv_ref, z_ref, ba_ref, alog_r_ref, dt_r_ref, alog_c_ref, dt_c_ref, nw_ref,
     o_ref, state, o_scr) = refs
    dh, nh, w = GDN_HEAD_DIM, GDN_HEADS, GDN_WIDTH
    pad = SUBLANE

    @pl.when(pl.program_id(1) == 0)
    def _():
        state[...] = jnp.zeros_like(state)

    yield
    qkv = qkv_ref[...]

    ba = ba_ref[...]
    beta_c = _sigmoid(ba)
    g_c = -jnp.exp(alog_r_ref[...]) * _softplus(ba + dt_r_ref[...])
    er = lax.broadcasted_iota(jnp.int32, (pad, LANE), 0)
    ec = lax.broadcasted_iota(jnp.int32, (pad, LANE), 1)
    pick = (er == ec).astype(BF16)
    ba_hi, ba_mid, ba_lo = _split3(ba)
    ba_t = _dg(pick, ba_hi, NT) + (_dg(pick, ba_mid, NT) + _dg(pick, ba_lo, NT))
    g_r = -jnp.exp(alog_c_ref[...]) * _softplus(ba_t + dt_c_ref[...])

    yield
    gw = nh * CHUNK
    rr = lax.broadcasted_iota(jnp.int32, (gw, gw), 0)
    cc = lax.broadcasted_iota(jnp.int32, (gw, gw), 1)
    same_blk = (rr // CHUNK) == (cc // CHUNK)
    u_bd = jnp.logical_and(same_blk, rr <= cc)
    tt = lax.broadcasted_iota(jnp.int32, (CHUNK, gw), 0)
    ss = lax.broadcasted_iota(jnp.int32, (CHUNK, gw), 1) % CHUNK
    ls_strict = tt > ss
    ls_incl = tt >= ss
    eye_ls = (tt == ss).astype(F32)
    l_incl = _tri(CHUNK, False)
    src = lax.broadcasted_iota(jnp.int32, (LANE, gw), 0)
    spread_g = src == nh + lax.broadcasted_iota(jnp.int32, (LANE, gw), 1) // CHUNK
    spread_b = src == lax.broadcasted_iota(jnp.int32, (LANE, gw), 1) // CHUNK
    srcw = lax.broadcasted_iota(jnp.int32, (LANE, w), 0)
    spread_gw = srcw == nh + lax.broadcasted_iota(jnp.int32, (LANE, w), 1) // dh
    spread_bw = srcw == lax.broadcasted_iota(jnp.int32, (LANE, w), 1) // dh
    head_rows = (lax.broadcasted_iota(jnp.int32, (gw, w), 0) // CHUNK
                 == lax.broadcasted_iota(jnp.int32, (gw, w), 1) // dh)

    def bd(xb):
        return _stack_bd(xb, same_blk, nh)

    def bd_wide(xb):
        return _stack_bd(xb, head_rows, nh)

    def ls_dot(a, y):
        return jnp.dot(a.astype(BF16), bd(y.astype(BF16)), preferred_element_type=F32)

    def l2norm_heads(x):
        parts = []
        for h in range(nh):
            xh = x[:, h * dh:(h + 1) * dh]
            parts.append(xh * lax.rsqrt(jnp.sum(xh * xh, axis=-1, keepdims=True) + EPS_L2))
        return jnp.concatenate(parts, axis=1)

    n_chunks = tc // CHUNK
    heads = [slice(h * dh, (h + 1) * dh) for h in range(nh)]
    d = []
    for c in range(n_chunks):
        rows = slice(c * CHUNK, (c + 1) * CHUNK)
        g_rows = g_c[rows]
        b_rows = beta_c[rows]
        cum_c = _sel_dot(l_incl, g_rows)
        acum = _dot_sel(cum_c, spread_g)
        cumw = _dot_sel(cum_c, spread_gw)
        beta = _dot_sel(b_rows, spread_b)
        betaw = _dot_sel(b_rows, spread_bw)
        g_row = jnp.concatenate([g_r[nh + h:nh + h + 1, rows] for h in range(nh)], axis=1)
        crow = _dot_sel(jnp.broadcast_to(g_row, (SUBLANE, gw)), u_bd)[0:1, :]
        decay = jnp.where(ls_incl, jnp.exp(jnp.where(ls_incl, acum - crow, 0.0)), 0.0)
        qc = l2norm_heads(qkv[rows, 0:w]) * (dh ** -0.5)
        kc = l2norm_heads(qkv[rows, w:2 * w])
        vc = qkv[rows, 2 * w:3 * w]
        kq = jnp.concatenate([kc, qc], axis=0).astype(BF16)
        m = _dg(kq, bd_wide(kc.astype(BF16)), NT)
        n = -jnp.where(ls_strict, m[:CHUNK] * decay * beta, 0.0)
        d.append(dict(n1=n, n=n, t=eye_ls + n, p=m[CHUNK:] * decay, cumw=cumw, betaw=betaw, qc=qc, kc=kc, vc=vc))
    yield
    for e in d:
        e['n'] = ls_dot(e['n'], e['n'])
    yield
    steps = CHUNK.bit_length() - 2
    for k in range(steps):
        for e in d:
            if k + 1 < steps:
                both = ls_dot(jnp.concatenate([e['t'], e['n']], axis=0), e['n'])
                e['t'] = e['t'] + both[:CHUNK]
                e['n'] = both[CHUNK:]
            else:
                e['t'] = e['t'] + ls_dot(e['t'], e['n'])
        yield
    for e in d:
        e['r'] = eye_ls - e['t'] + _dot3_bd(e['n1'], e['t'], bd)
    yield
    for e in d:
        e['t'] = e['t'] + ls_dot(e['t'], e['r'])
    yield
    for e in d:
        e_c = jnp.exp(e['cumw'])
        clast = e['cumw'][CHUNK - 1:CHUNK, :]
        tb = e['t'].astype(BF16)
        e['u'] = _dot(tb, bd_wide((e['vc'] * e['betaw']).astype(BF16)))
        e['wm'] = _dot(tb, bd_wide((e['kc'] * (e['betaw'] * e_c)).astype(BF16)))
        e['q_dec'] = e['qc'] * e_c
        e['k_dec'] = e['kc'] * jnp.exp(clast - e['cumw'])
        e['g'] = jnp.exp(clast)
    yield
    for e in d:
        pb = e['p'].astype(BF16)
        e['qe'] = e['q_dec'] - _dot(pb, bd_wide(e['wm'].astype(BF16)))
        e['o0'] = _dot(pb, bd_wide(e['u'].astype(BF16)))
        e['kw'] = [_bdg(e['k_dec'][:, ln], e['wm'][:, ln], TN) for ln in heads]
        e['n0'] = [_bdg(e['k_dec'][:, ln], e['u'][:, ln], TN) for ln in heads]
    yield
    for c, e in enumerate(d):
        outs = []
        for h, ln in enumerate(heads):
            s0 = state[h]
            lhs = jnp.concatenate([e['qe'][:, ln], e['kw'][h]], axis=0).astype(BF16)
            both = _dot(lhs, s0.astype(BF16))
            outs.append(both[:CHUNK] + e['o0'][:, ln])
            state[h] = s0 * e['g'][:, ln] - both[CHUNK:] + e['n0'][h]
        o_scr[c * CHUNK:(c + 1) * CHUNK, :] = jnp.concatenate(outs, axis=1)
    yield

    zs = _silu(z_ref[...])
    for h in range(nh):
        ln = slice(h * dh, (h + 1) * dh)
        o = o_scr[:, ln]
        o = o * lax.rsqrt(jnp.mean(o * o, axis=-1, keepdims=True) + EPS_RMS) * nw_ref[...]
        o_ref[:, ln] = o * zs[:, ln]


def _mixer_kernel(*refs, n_in_r, n_in_g, n_out_r, has_vres, tc):
    refs = list(refs)
    in_r, refs = refs[:n_in_r], refs[n_in_r:]
    in_g, refs = refs[:n_in_g], refs[n_in_g:]
    out_r, refs = refs[:n_out_r], refs[n_out_r:]
    out_g, refs = refs[:1], refs[1:]
    scr_r, scr_g = refs[:2], refs[2:]
    bodies = [_rwkv_body(in_r + out_r + scr_r, has_vres, tc), _gdn_body(in_g + out_g + scr_g, tc)]
    while bodies:
        for body in list(bodies):
            if next(body, bodies) is bodies:
                bodies.remove(body)


def _mixer(y_r, rparams, y_qkv, y_z, y_ba, gparams, seq, y_vres=None, v_first=None, vres_params=None, tc=512):
    t = y_r.shape[0]
    tc = min(tc, seq)
    per_b = seq // tc
    nb = t // seq
    has_vres = y_vres is not None
    w = RWKV_WIDTH
    row = lambda b, s: (b * per_b + s, 0)
    once = dict(index_map=lambda b, s: (0, 0), pipeline_mode=pl.Buffered(1))
    tile = lambda width: pl.BlockSpec((tc, width), row)

    in_r = [y_r] + ([y_vres, v_first] if has_vres else [])
    spec_r = [tile(a.shape[1]) for a in in_r]
    plist = list(rparams) + (list(vres_params) if has_vres else [])
    in_r += plist
    spec_r += [pl.BlockSpec(p.shape, **once) for p in plist]
    in_g = [y_qkv, y_z, y_ba] + list(gparams)
    spec_g = [tile(GDN_QKV), tile(GDN_WIDTH), tile(LANE)] + [pl.BlockSpec(p.shape, **once) for p in gparams]

    n_out_r = 1 if has_vres else 2
    n_out = n_out_r + 1
    scratch = [pltpu.VMEM((RWKV_WIDTH // MXU_TILE, MXU_TILE, MXU_TILE), F32), pltpu.VMEM((tc, w), F32),
               pltpu.VMEM((GDN_HEADS, GDN_HEAD_DIM, GDN_HEAD_DIM), F32), pltpu.VMEM((tc, GDN_WIDTH), F32)]
    return pl.pallas_call(
        functools.partial(_mixer_kernel, n_in_r=len(in_r), n_in_g=len(in_g), n_out_r=n_out_r,
                          has_vres=has_vres, tc=tc),
        grid=(nb, per_b),
        in_specs=spec_r + spec_g,
        out_specs=[tile(w)] * n_out,
        out_shape=[jax.ShapeDtypeStruct((t, w), F32)] * n_out,
        scratch_shapes=scratch,
        compiler_params=pltpu.CompilerParams(
            dimension_semantics=("parallel", "arbitrary"), vmem_limit_bytes=VMEM_LIMIT),
        name="mixer_vres" if has_vres else "mixer",
    )(*in_r, *in_g)


def _pad_cols(m, width):
    return jnp.pad(m, ((0, 0), (0, width - m.shape[1])))


def kernel(x, c, norm_gain, ada_w, ada_b, ffn_w_gu, ffn_w_down, w_in, w_out, rwkv_mu, rwkv_w0, rwkv_w_up, rwkv_a0, rwkv_a_up, rwkv_g_up, rwkv_k_k, rwkv_k_a, rwkv_r_k, rwkv_gn_w, rwkv_gn_b, vres_w_down, vres_mu, vres_w_up, vres_v0, gdn_conv_w, gdn_a_log, gdn_dt_bias, gdn_norm_w, final_gain):
    bsz, seq, d = x.shape
    depth = norm_gain.shape[0]
    nh = GDN_HEADS
    mod = _adaln(c, ada_w, ada_b)
    xf = x.reshape(bsz * seq, d)
    v_first = None
    row = lambda p: p.reshape(1, -1)
    for l in range(depth):
        def mods(sub):
            m = mod[l * 3 + sub]
            return tuple(m[:, i * d:(i + 1) * d].reshape(bsz, 1, d) for i in range(3))

        shift, scale, gate = mods(0)
        xf = _ffn(xf, norm_gain[l, 0], shift, scale, gate,
                  ffn_w_gu[l, 0].astype(BF16), ffn_w_down[l, 0].astype(BF16), seq)

        shift, scale, gate = mods(1)
        wl = w_in[l]
        w_ba = _pad_cols(wl[:, RWKV_COLS + 4 * GDN_WIDTH:], LANE)
        weights = [wl[:, :RWKV_COLS], wl[:, RWKV_COLS:RWKV_COLS + GDN_QKV],
                   wl[:, RWKV_COLS + GDN_QKV:RWKV_COLS + 4 * GDN_WIDTH], w_ba]
        if l > 0:
            weights.append(_pad_cols(vres_w_down[l - 1], LANE))
        ys = _inproj(xf, norm_gain[l, 1], shift, scale, row(rwkv_mu[l]), gdn_conv_w[l],
                     [wt.astype(BF16) for wt in weights], seq,
                     vres_mu=_pad_cols(row(vres_mu[l - 1]), LANE) if l > 0 else None)
        y_r, y_qkv, y_z, y_ba = ys[:4]

        rparams = [row(rwkv_w0[l]), rwkv_w_up[l], row(rwkv_a0[l]), rwkv_a_up[l],
                   rwkv_g_up[l], row(rwkv_k_k[l]), row(rwkv_k_a[l]), row(rwkv_r_k[l]),
                   row(rwkv_gn_w[l]), row(rwkv_gn_b[l])]
        alog = gdn_a_log[l].astype(F32)
        dtb = gdn_dt_bias[l].astype(F32)
        place = lambda p: jnp.pad(p, (nh, LANE - 2 * nh))
        gparams = [row(place(alog)), row(place(dtb)),
                   place(alog)[:SUBLANE].reshape(SUBLANE, 1), place(dtb)[:SUBLANE].reshape(SUBLANE, 1),
                   row(gdn_norm_w[l])]
        if l == 0:
            o_r, v_first, o_g = _mixer(y_r, rparams, y_qkv, y_z, y_ba, gparams, seq)
        else:
            vparams = [jnp.pad(vres_w_up[l - 1], ((0, LANE - VRES_RANK), (0, 0))),
                       row(vres_v0[l - 1])]
            o_r, o_g = _mixer(y_r, rparams, y_qkv, y_z, y_ba, gparams, seq,
                              y_vres=ys[4], v_first=v_first, vres_params=vparams)

        wo = w_out[l].astype(BF16)
        mix = (gate, o_r, o_g, wo[:RWKV_WIDTH], wo[RWKV_WIDTH:])
        shift, scale, gate = mods(2)
        xf = _ffn(xf, norm_gain[l, 2], shift, scale, gate,
                  ffn_w_gu[l, 1].astype(BF16), ffn_w_down[l, 1].astype(BF16), seq,
                  final_gain=final_gain if l == depth - 1 else None, mix=mix)
    return xf.reshape(bsz, seq, d)
```

```python
import functools
import math

import jax
import jax.numpy as jnp
from jax import lax
from jax.experimental import pallas as pl
from jax.experimental.pallas import tpu as pltpu

F32 = jnp.float32
BF16 = jnp.bfloat16

RWKV_HEAD_DIM = 64
RWKV_HEADS = 8
RWKV_WIDTH = RWKV_HEAD_DIM * RWKV_HEADS
GDN_HEAD_DIM = 128
GDN_HEADS = 4
GDN_WIDTH = GDN_HEAD_DIM * GDN_HEADS
DECAY_RANK = 64
ICLR_RANK = 64
VRES_RANK = 32
GATE_RANK = 128
RWKV_COLS = 3 * RWKV_WIDTH + DECAY_RANK + ICLR_RANK + GATE_RANK
GDN_QKV = 3 * GDN_WIDTH
CONV_WIDTH = 4
CHUNK = 64
LANE = 128
SUBLANE = 8
MXU_TILE = 256
assert CHUNK == RWKV_HEAD_DIM and MXU_TILE % RWKV_HEAD_DIM == 0 and RWKV_WIDTH % MXU_TILE == 0
assert GDN_HEADS * CHUNK == MXU_TILE
EPS_RMS = 1e-6
EPS_GN = 64e-5
EPS_L2 = 1e-6
MACARON_WEIGHT = 0.5
VMEM_LIMIT = 52 * 1024 * 1024

NT = (((1,), (1,)), ((), ()))
TN = (((0,), (0,)), ((), ()))


def _sigmoid(x):
    return 1.0 / (1.0 + jnp.exp(-x))


def _softplus(x):
    return jnp.maximum(x, 0.0) + jnp.log(1.0 + jnp.exp(-jnp.abs(x)))


def _silu(x):
    return x * _sigmoid(x)


def _dot(a, b, precision=None):
    return jnp.dot(a, b, preferred_element_type=F32, precision=precision)


def _dg(a, b, dims, precision=None):
    return lax.dot_general(a, b, dims, preferred_element_type=F32, precision=precision)


def _bdg(a, b, dims):
    return lax.dot_general(a.astype(BF16), b.astype(BF16), dims, preferred_element_type=F32)


def _split2(a):
    hi = a.astype(BF16)
    return hi, (a - hi.astype(F32)).astype(BF16)


def _stack_bd(xb, mask, reps):
    return jnp.where(mask, jnp.concatenate([xb] * reps, axis=0), jnp.zeros((), BF16))


def _dot3_bd(a, y, stack):
    ah, al = _split2(a)
    yh, yl = _split2(y)
    bh, bl = stack(yh), stack(yl)
    d = lambda p, q: jnp.dot(p, q, preferred_element_type=F32)
    rows = a.shape[0]
    both = d(jnp.concatenate([ah, al], axis=0), bh)
    return both[:rows] + (both[rows:] + d(ah, bl))


def _split3(a):
    hi = a.astype(BF16)
    r1 = a - hi.astype(F32)
    mid = r1.astype(BF16)
    return hi, mid, (r1 - mid.astype(F32)).astype(BF16)


def _sel_dot(sel, x):
    sb = sel.astype(BF16)
    hi, mid, lo = _split3(x)
    d = lambda y: jnp.dot(sb, y, preferred_element_type=F32)
    return d(hi) + (d(mid) + d(lo))


def _dot_sel(x, sel):
    sb = sel.astype(BF16)
    hi, mid, lo = _split3(x)
    d = lambda y: jnp.dot(y, sb, preferred_element_type=F32)
    return d(hi) + (d(mid) + d(lo))


def _norm_mod(x, gain, shift, scale):
    ms = jnp.mean(x * x, axis=-1, keepdims=True)
    y = x * lax.rsqrt(ms + EPS_RMS) * gain
    return y * (1.0 + scale) + shift


def _tri(n, strict):
    r = lax.broadcasted_iota(jnp.int32, (n, n), 0)
    c = lax.broadcasted_iota(jnp.int32, (n, n), 1)
    return (r > c) if strict else (r >= c)


def _mod_kernel(c_ref, w_ref, b_ref, o_ref):
    ch, cl = _split2(_silu(c_ref[...]))
    wh, wl = _split2(w_ref[0])
    o_ref[0] = _dot(ch, wh) + (_dot(ch, wl) + _dot(cl, wh)) + b_ref[0]


def _adaln(c, ada_w, ada_b):
    n_l, n_sub, d, d3 = ada_w.shape
    b = c.shape[0]
    rows = -(-b // SUBLANE) * SUBLANE
    c_pad = jnp.pad(c, ((0, rows - b), (0, 0)))
    tn = d3
    out = pl.pallas_call(
        _mod_kernel,
        grid=(n_l * n_sub, d3 // tn),
        in_specs=[
            pl.BlockSpec((rows, d), lambda i, j: (0, 0)),
            pl.BlockSpec((1, d, tn), lambda i, j: (i, 0, j)),
            pl.BlockSpec((1, 1, tn), lambda i, j: (i, 0, j)),
        ],
        out_specs=pl.BlockSpec((1, rows, tn), lambda i, j: (i, 0, j)),
        out_shape=jax.ShapeDtypeStruct((n_l * n_sub, rows, d3), F32),
        compiler_params=pltpu.CompilerParams(
            dimension_semantics=("parallel", "parallel"), vmem_limit_bytes=VMEM_LIMIT),
        name="adaln_mod",
    )(c_pad, ada_w.reshape(n_l * n_sub, d, d3), ada_b.reshape(n_l * n_sub, 1, d3))
    return out[:, :b, :]


def _ffn_kernel(*refs, final, mix, sub):
    refs = list(refs)
    x_ref, gain_ref, sh_ref, sc_ref, gt_ref, wgu_ref, wd_ref = refs[:7]
    del refs[:7]
    if mix:
        gm_ref, or_ref, og_ref, w1_ref, w2_ref = refs[:5]
        del refs[:5]
    if final:
        fg_ref = refs.pop(0)
    (o_ref,) = refs

    x = x_ref[...]
    if mix:
        mixed = _dot(or_ref[...].astype(BF16), w1_ref[...]) + _dot(og_ref[...].astype(BF16), w2_ref[...])
        x = x + gm_ref[0] * mixed
    h = _norm_mod(x, gain_ref[...], sh_ref[0], sc_ref[0]).astype(BF16)

    f = wd_ref.shape[0]
    blocks = [(a, min(a + sub, f)) for a in range(0, f, sub)]
    gu = [(_dot(h, wgu_ref[:, a:b]), _dot(h, wgu_ref[:, f + a:f + b])) for a, b in blocks]
    down = None
    for (a, b), (g, u) in zip(blocks, gu):
        part = _dot((_silu(g) * u).astype(BF16), wd_ref[a:b, :])
        down = part if down is None else down + part

    y = x + (MACARON_WEIGHT * gt_ref[0]) * down
    if final:
        ms = jnp.mean(y * y, axis=-1, keepdims=True)
        y = y * lax.rsqrt(ms + EPS_RMS) * fg_ref[...]
    o_ref[...] = y


def _ffn(x, gain, shift, scale, gate, w_gu, w_dn, which, seq, final_gain=None, mix=None, tm=512,
         sub=3 * MXU_TILE):
    t, d = x.shape
    tm = min(tm, seq)
    per_b = seq // tm
    final = final_gain is not None
    row = lambda i: (i, 0)
    once = dict(index_map=lambda i: (0, 0), pipeline_mode=pl.Buffered(1))
    modspec = pl.BlockSpec((1, 1, d), lambda i: (i // per_b, 0, 0))
    in_specs = [
        pl.BlockSpec((tm, d), row),
        pl.BlockSpec((1, d), **once),
        modspec, modspec, modspec,
        pl.BlockSpec((None, None) + w_gu.shape[2:], lambda i: which + (0, 0), pipeline_mode=pl.Buffered(1)),
        pl.BlockSpec((None, None) + w_dn.shape[2:], lambda i: which + (0, 0), pipeline_mode=pl.Buffered(1)),
    ]
    args = [x, gain.reshape(1, d), shift, scale, gate, w_gu, w_dn]
    if mix is not None:
        gate_mix, o_r, o_g, w1, w2 = mix
        in_specs += [modspec, pl.BlockSpec((tm, o_r.shape[1]), row), pl.BlockSpec((tm, o_g.shape[1]), row),
                     pl.BlockSpec(w1.shape, **once), pl.BlockSpec(w2.shape, **once)]
        args += [gate_mix, o_r, o_g, w1, w2]
    if final:
        in_specs.append(pl.BlockSpec((1, d), **once))
        args.append(final_gain.reshape(1, d))
    return pl.pallas_call(
        functools.partial(_ffn_kernel, final=final, mix=mix is not None, sub=sub),
        grid=(t // tm,),
        in_specs=in_specs,
        out_specs=pl.BlockSpec((tm, d), row),
        out_shape=jax.ShapeDtypeStruct((t, d), F32),
        compiler_params=pltpu.CompilerParams(
            dimension_semantics=("parallel",), vmem_limit_bytes=VMEM_LIMIT),
        name="ffn" + ("_mix" if mix is not None else "") + ("_final" if final else ""),
    )(*args)


def _shift_rows(y, carry_row):
    rolled = pltpu.roll(y, 1, 0)
    first = lax.broadcasted_iota(jnp.int32, y.shape, 0) == 0
    return jnp.where(first, carry_row, rolled)


def _inproj_kernel(*refs, has_vres, per_b, tm, col):
    refs = list(refs)
    x_ref, gain_ref, sh_ref, sc_ref, mu_ref, cw_ref, wr_ref, wq_ref, wz_ref, wb_ref = refs[:10]
    del refs[:10]
    if has_vres:
        vmu_ref, wv_ref = refs[:2]
        del refs[:2]
    or_ref, oq_ref, oz_ref, ob_ref = refs[:4]
    del refs[:4]
    if has_vres:
        ov_ref = refs.pop(0)
    carry_r, ext = refs[:2]
    carry_v = refs[2] if has_vres else None
    pad = SUBLANE

    @pl.when(pl.program_id(0) % per_b == 0)
    def _():
        carry_r[...] = jnp.zeros_like(carry_r)
        ext[0:pad, :] = jnp.zeros((pad, GDN_QKV), F32)
        if has_vres:
            carry_v[...] = jnp.zeros_like(carry_v)

    h = _norm_mod(x_ref[...], gain_ref[...], sh_ref[0], sc_ref[0]).astype(BF16)
    blocks = lambda width: [(a, min(a + col, width)) for a in range(0, width, col)]
    for a, b in blocks(GDN_QKV):
        q = _dot(h, wq_ref[:, a:b])
        ext[pad:pad + tm, a:b] = q
        conv = q * cw_ref[CONV_WIDTH - 1:CONV_WIDTH, a:b]
        for j in range(CONV_WIDTH - 1):
            back = CONV_WIDTH - 1 - j
            conv = conv + ext[pad - back:pad - back + tm, a:b] * cw_ref[j:j + 1, a:b]
        ext[0:pad, a:b] = q[tm - pad:tm, :]
        oq_ref[:, a:b] = _silu(conv)

    for a, b in blocks(RWKV_COLS):
        y = _dot(h, wr_ref[:, a:b])
        prev = _shift_rows(y, carry_r[0:1, a:b])
        carry_r[0:1, a:b] = y[tm - 1:tm, :]
        or_ref[:, a:b] = y + (prev - y) * mu_ref[:, a:b]
    if has_vres:
        yv = _dot(h, wv_ref[...])
        prev_v = _shift_rows(yv, carry_v[0:1, :])
        carry_v[0:1, :] = yv[tm - 1:tm, :]
        ov_ref[...] = yv + (prev_v - yv) * vmu_ref[...]

    oz_ref[...] = _dot(h, wz_ref[...])
    ob_ref[...] = _dot(h, wb_ref[...])


def _inproj(x, gain, shift, scale, mu, conv_w, weights, seq, vres_mu=None, tm=512, col=2 * MXU_TILE):
    t, d = x.shape
    tm = min(tm, seq)
    per_b = seq // tm
    has_vres = vres_mu is not None
    row = lambda i: (i, 0)
    once = dict(index_map=lambda i: (0, 0), pipeline_mode=pl.Buffered(1))
    modspec = pl.BlockSpec((1, 1, d), lambda i: (i // per_b, 0, 0))
    small = [gain.reshape(1, d), shift, scale, mu, conv_w]
    in_specs = [pl.BlockSpec((tm, d), row), pl.BlockSpec((1, d), **once), modspec, modspec,
                pl.BlockSpec(mu.shape, **once), pl.BlockSpec(conv_w.shape, **once)]
    in_specs += [pl.BlockSpec(w.shape, **once) for w in weights[:4]]
    args = [x] + small + list(weights[:4])
    outs = list(weights[:4])
    scratch = [pltpu.VMEM((SUBLANE, RWKV_COLS), F32), pltpu.VMEM((tm + SUBLANE, GDN_QKV), F32)]
    if has_vres:
        in_specs += [pl.BlockSpec(vres_mu.shape, **once), pl.BlockSpec(weights[4].shape, **once)]
        args += [vres_mu, weights[4]]
        outs.append(weights[4])
        scratch.append(pltpu.VMEM((SUBLANE, LANE), F32))
    return pl.pallas_call(
        functools.partial(_inproj_kernel, has_vres=has_vres, per_b=per_b, tm=tm, col=col),
        grid=(t // tm,),
        in_specs=in_specs,
        out_specs=[pl.BlockSpec((tm, w.shape[1]), row) for w in outs],
        out_shape=[jax.ShapeDtypeStruct((t, w.shape[1]), F32) for w in outs],
        scratch_shapes=scratch,
        compiler_params=pltpu.CompilerParams(
            dimension_semantics=("arbitrary",), vmem_limit_bytes=VMEM_LIMIT),
        name="inproj",
    )(*args)


def _rwkv_body(refs, has_vres, tc):
    if has_vres:
        (y_ref, yv_ref, vf_ref, w0_ref, wup_ref, a0_ref, aup_ref, gup_ref, kk_ref, ka_ref,
         rk_ref, gnw_ref, gnb_ref, vup_ref, v0_ref,
         o_ref, state, o_scr) = refs
    else:
        (y_ref, w0_ref, wup_ref, a0_ref, aup_ref, gup_ref, kk_ref, ka_ref,
         rk_ref, gnw_ref, gnb_ref,
         o_ref, v_ref, state, o_scr) = refs
    n, w = RWKV_HEAD_DIM, RWKV_WIDTH

    @pl.when(pl.program_id(1) == 0)
    def _():
        state[...] = jnp.zeros_like(state)

    yield
    ym = y_ref[...]
    r = ym[:, 0:w]
    k = ym[:, w:2 * w]
    v = ym[:, 2 * w:3 * w]
    wd = ym[:, 3 * w:3 * w + DECAY_RANK]
    ad = ym[:, 3 * w + DECAY_RANK:3 * w + DECAY_RANK + ICLR_RANK]
    gd = ym[:, 3 * w + DECAY_RANK + ICLR_RANK:]

    if has_vres:
        logit = v0_ref[...] + _dot(yv_ref[...].astype(BF16), vup_ref[...].astype(BF16))
        v = v + (vf_ref[...] - v) * _sigmoid(logit)
    else:
        v_ref[...] = v

    z = w0_ref[...] + _dot(jnp.tanh(wd).astype(BF16), wup_ref[...].astype(BF16))
    logw = -math.exp(-0.5) * _sigmoid(z)
    a = _sigmoid(a0_ref[...] + _dot(ad.astype(BF16), aup_ref[...].astype(BF16)))
    g = _dot(_sigmoid(gd).astype(BF16), gup_ref[...].astype(BF16))

    gw = MXU_TILE
    hpg = gw // n
    groups = [slice(gi * gw, (gi + 1) * gw) for gi in range(w // gw)]
    rr = lax.broadcasted_iota(jnp.int32, (gw, gw), 0)
    cc = lax.broadcasted_iota(jnp.int32, (gw, gw), 1)
    same_blk = (rr // CHUNK) == (cc // CHUNK)
    head_ones = (rr // n) == (cc // n)
    tt = lax.broadcasted_iota(jnp.int32, (CHUNK, gw), 0)
    ss = lax.broadcasted_iota(jnp.int32, (CHUNK, gw), 1) % CHUNK
    ls_strict = tt > ss
    ls_incl = tt >= ss
    eye_ls = (tt == ss).astype(F32)
    l_incl = _tri(CHUNK, False)

    def head_sum(x):
        hi, lo = _split2(x)
        rows = x.shape[0]
        parts = [p[:, ln] for p in (hi, lo) for ln in groups]
        r = jnp.dot(jnp.concatenate(parts, axis=0), head_ones.astype(BF16), preferred_element_type=F32)
        ng = len(groups)
        return jnp.concatenate([r[gi * rows:(gi + 1) * rows] + r[(ng + gi) * rows:(ng + gi + 1) * rows]
                                for gi in range(ng)], axis=1)

    def bd(x):
        return _stack_bd(x.astype(BF16), same_blk, hpg)

    def ls_dot(a, b_bd):
        return jnp.dot(a.astype(BF16), b_bd, preferred_element_type=F32)

    yield
    kkr = k * kk_ref[...]
    kk = kkr * lax.rsqrt(head_sum(kkr * kkr) + EPS_L2)
    k2 = k * (1.0 + (a - 1.0) * ka_ref[...])
    b = kk * a
    av = -kk

    n_chunks = tc // CHUNK
    insts = [(c, gi) for c in range(n_chunks) for gi in range(len(groups))]
    d = {}
    for c in range(n_chunks):
        rows = slice(c * CHUNK, (c + 1) * CHUNK)
        lw = logw[rows]
        cum = _sel_dot(l_incl, lw)
        last = cum[CHUNK - 1:CHUNK, :]
        e_pos = jnp.exp(cum)
        e_neg = jnp.exp(-cum)
        e_to_end = jnp.exp(last - cum)
        gam_end = jnp.exp(last)
        at = av[rows] * jnp.exp(cum - lw)
        rt = r[rows] * e_pos
        bt = b[rows] * e_neg
        kt = k2[rows] * e_neg
        bg = b[rows] * e_to_end
        kg = k2[rows] * e_to_end
        vc = v[rows]
        for gi, ln in enumerate(groups):
            d[c, gi] = dict(at=at[:, ln], rt=rt[:, ln], bt=bt[:, ln], kt=kt[:, ln], bg=bg[:, ln],
                            kg=kg[:, ln], v=vc[:, ln], gam=gam_end[:, ln])
    yield
    for i in insts:
        e = d[i]
        x_ar = jnp.concatenate([e['at'], e['rt']], axis=0).astype(BF16)
        y_bk = jnp.concatenate([bd(e['bt']), bd(e['kt'])], axis=0)
        m1 = _dg(x_ar, y_bk, NT)
        e['n'] = jnp.where(ls_strict, m1[:CHUNK, :gw], 0.0)
        e['a_ak'] = jnp.where(ls_strict, m1[:CHUNK, gw:], 0.0)
        e['a_rb'] = jnp.where(ls_incl, m1[CHUNK:, :gw], 0.0)
        e['a_rk'] = jnp.where(ls_incl, m1[CHUNK:, gw:], 0.0)
        e['t'] = eye_ls + e['n']
        e['v_bd'] = bd(e['v'])
    yield
    for i in insts:
        d[i]['n'] = ls_dot(d[i]['n'], bd(d[i]['n']))
    yield
    steps = CHUNK.bit_length() - 2
    for k in range(steps):
        for i in insts:
            e = d[i]
            if k + 1 < steps:
                both = ls_dot(jnp.concatenate([e['t'], e['n']], axis=0), bd(e['n']))
                e['t'] = e['t'] + both[:CHUNK]
                e['n'] = both[CHUNK:]
            else:
                e['t'] = e['t'] + ls_dot(e['t'], bd(e['n']))
        yield
    for i in insts:
        e = d[i]
        e['w'] = ls_dot(e['t'], bd(e['at']))
        both = ls_dot(jnp.concatenate([e['a_ak'], e['a_rk']], axis=0), e['v_bd'])
        e['akv'] = both[:CHUNK]
        e['arkv'] = both[CHUNK:]
    yield
    for i in insts:
        e = d[i]
        e['u0'] = ls_dot(e['t'], bd(e['akv']))
        e['q'] = e['rt'] + ls_dot(e['a_rb'], bd(e['w']))
    yield
    for i in insts:
        e = d[i]
        e['o0'] = ls_dot(e['a_rb'], bd(e['u0'])) + e['arkv']
        e['m'] = jnp.where(same_blk, _bdg(e['w'], e['bg'], TN), 0.0)
        uv = jnp.concatenate([e['u0'], e['v']], axis=0)
        bk_end = jnp.concatenate([e['bg'], e['kg']], axis=0)
        e['n0'] = jnp.where(same_blk, _bdg(uv, bk_end, TN), 0.0)
    yield
    for c, gi in insts:
        e = d[c, gi]
        s0 = state[gi]
        sb = s0.astype(BF16)
        o_scr[c * CHUNK:(c + 1) * CHUNK, groups[gi]] = _dg(e['q'].astype(BF16), sb, NT) + e['o0']
        state[gi] = s0 * e['gam'] + jnp.dot(sb, e['m'].astype(BF16), preferred_element_type=F32) + e['n0']
    yield

    o = o_scr[...]
    inv_n = 1.0 / n
    mean = head_sum(o) * inv_n
    dev = o - mean
    var = head_sum(dev * dev) * inv_n
    on = dev * lax.rsqrt(var + EPS_GN) * gnw_ref[...] + gnb_ref[...]
    bonus = head_sum(r * k2 * rk_ref[...])
    o_ref[...] = (on + bonus * v) * g


def _gdn_body(refs, tc):
    (qkv_ref, z_ref, ba_ref, alog_r_ref, dt_r_ref, alog_c_ref, dt_c_ref, nw_ref,
     o_ref, state, o_scr) = refs
    dh, nh, w = GDN_HEAD_DIM, GDN_HEADS, GDN_WIDTH
    pad = SUBLANE

    @pl.when(pl.program_id(1) == 0)
    def _():
        state[...] = jnp.zeros_like(state)

    yield
    qkv = qkv_ref[...]

    ba = ba_ref[...]
    beta_c = _sigmoid(ba)
    g_c = -jnp.exp(alog_r_ref[...]) * _softplus(ba + dt_r_ref[...])
    er = lax.broadcasted_iota(jnp.int32, (pad, LANE), 0)
    ec = lax.broadcasted_iota(jnp.int32, (pad, LANE), 1)
    pick = (er == ec).astype(BF16)
    ba_hi, ba_mid, ba_lo = _split3(ba)
    ba_t = _dg(pick, ba_hi, NT) + (_dg(pick, ba_mid, NT) + _dg(pick, ba_lo, NT))
    g_r = -jnp.exp(alog_c_ref[...]) * _softplus(ba_t + dt_c_ref[...])

    yield
    gw = nh * CHUNK
    rr = lax.broadcasted_iota(jnp.int32, (gw, gw), 0)
    cc = lax.broadcasted_iota(jnp.int32, (gw, gw), 1)
    same_blk = (rr // CHUNK) == (cc // CHUNK)
    u_bd = jnp.logical_and(same_blk, rr <= cc)
    tt = lax.broadcasted_iota(jnp.int32, (CHUNK, gw), 0)
    ss = lax.broadcasted_iota(jnp.int32, (CHUNK, gw), 1) % CHUNK
    ls_strict = tt > ss
    ls_incl = tt >= ss
    eye_ls = (tt == ss).astype(F32)
    l_incl = _tri(CHUNK, False)
    src = lax.broadcasted_iota(jnp.int32, (LANE, gw), 0)
    spread_g = src == nh + lax.broadcasted_iota(jnp.int32, (LANE, gw), 1) // CHUNK
    spread_b = src == lax.broadcasted_iota(jnp.int32, (LANE, gw), 1) // CHUNK
    srcw = lax.broadcasted_iota(jnp.int32, (LANE, w), 0)
    spread_gw = srcw == nh + lax.broadcasted_iota(jnp.int32, (LANE, w), 1) // dh
    spread_bw = srcw == lax.broadcasted_iota(jnp.int32, (LANE, w), 1) // dh
    head_rows = (lax.broadcasted_iota(jnp.int32, (gw, w), 0) // CHUNK
                 == lax.broadcasted_iota(jnp.int32, (gw, w), 1) // dh)

    def bd(xb):
        return _stack_bd(xb, same_blk, nh)

    def bd_wide(xb):
        return _stack_bd(xb, head_rows, nh)

    def ls_dot(a, y):
        return jnp.dot(a.astype(BF16), bd(y.astype(BF16)), preferred_element_type=F32)

    def l2norm_heads(x):
        parts = []
        for h in range(nh):
            xh = x[:, h * dh:(h + 1) * dh]
            parts.append(xh * lax.rsqrt(jnp.sum(xh * xh, axis=-1, keepdims=True) + EPS_L2))
        return jnp.concatenate(parts, axis=1)

    n_chunks = tc // CHUNK
    heads = [slice(h * dh, (h + 1) * dh) for h in range(nh)]
    d = []
    for c in range(n_chunks):
        rows = slice(c * CHUNK, (c + 1) * CHUNK)
        g_rows = g_c[rows]
        b_rows = beta_c[rows]
        cum_c = _sel_dot(l_incl, g_rows)
        acum = _dot_sel(cum_c, spread_g)
        cumw = _dot_sel(cum_c, spread_gw)
        beta = _dot_sel(b_rows, spread_b)
        betaw = _dot_sel(b_rows, spread_bw)
        g_row = jnp.concatenate([g_r[nh + h:nh + h + 1, rows] for h in range(nh)], axis=1)
        crow = _dot_sel(jnp.broadcast_to(g_row, (SUBLANE, gw)), u_bd)[0:1, :]
        decay = jnp.where(ls_incl, jnp.exp(jnp.where(ls_incl, acum - crow, 0.0)), 0.0)
        qc = l2norm_heads(qkv[rows, 0:w]) * (dh ** -0.5)
        kc = l2norm_heads(qkv[rows, w:2 * w])
        vc = qkv[rows, 2 * w:3 * w]
        kq = jnp.concatenate([kc, qc], axis=0).astype(BF16)
        m = _dg(kq, bd_wide(kc.astype(BF16)), NT)
        n = -jnp.where(ls_strict, m[:CHUNK] * decay * beta, 0.0)
        d.append(dict(n1=n, n=n, t=eye_ls + n, p=m[CHUNK:] * decay, cumw=cumw, betaw=betaw, qc=qc, kc=kc, vc=vc))
    yield
    for e in d:
        e['n'] = ls_dot(e['n'], e['n'])
    yield
    steps = CHUNK.bit_length() - 2
    for k in range(steps):
        for e in d:
            if k + 1 < steps:
                both = ls_dot(jnp.concatenate([e['t'], e['n']], axis=0), e['n'])
                e['t'] = e['t'] + both[:CHUNK]
                e['n'] = both[CHUNK:]
            else:
                e['t'] = e['t'] + ls_dot(e['t'], e['n'])
        yield
    for e in d:
        e['r'] = eye_ls - e['t'] + _dot3_bd(e['n1'], e['t'], bd)
    yield
    for e in d:
        e['t'] = e['t'] + ls_dot(e['t'], e['r'])
    yield
    for e in d:
        e_c = jnp.exp(e['cumw'])
        clast = e['cumw'][CHUNK - 1:CHUNK, :]
        tb = e['t'].astype(BF16)
        e['u'] = _dot(tb, bd_wide((e['vc'] * e['betaw']).astype(BF16)))
        e['wm'] = _dot(tb, bd_wide((e['kc'] * (e['betaw'] * e_c)).astype(BF16)))
        e['q_dec'] = e['qc'] * e_c
        e['k_dec'] = e['kc'] * jnp.exp(clast - e['cumw'])
        e['g'] = jnp.exp(clast)
    yield
    for e in d:
        pb = e['p'].astype(BF16)
        e['qe'] = e['q_dec'] - _dot(pb, bd_wide(e['wm'].astype(BF16)))
        e['o0'] = _dot(pb, bd_wide(e['u'].astype(BF16)))
        e['kw'] = [_bdg(e['k_dec'][:, ln], e['wm'][:, ln], TN) for ln in heads]
        e['n0'] = [_bdg(e['k_dec'][:, ln], e['u'][:, ln], TN) for ln in heads]
    yield
    for c, e in enumerate(d):
        outs = []
        for h, ln in enumerate(heads):
            s0 = state[h]
            lhs = jnp.concatenate([e['qe'][:, ln], e['kw'][h]], axis=0).astype(BF16)
            both = _dot(lhs, s0.astype(BF16))
            outs.append(both[:CHUNK] + e['o0'][:, ln])
            state[h] = s0 * e['g'][:, ln] - both[CHUNK:] + e['n0'][h]
        o_scr[c * CHUNK:(c + 1) * CHUNK, :] = jnp.concatenate(outs, axis=1)
    yield

    zs = _silu(z_ref[...])
    for h in range(nh):
        ln = slice(h * dh, (h + 1) * dh)
        o = o_scr[:, ln]
        o = o * lax.rsqrt(jnp.mean(o * o, axis=-1, keepdims=True) + EPS_RMS) * nw_ref[...]
        o_ref[:, ln] = o * zs[:, ln]


def _mixer_kernel(*refs, n_in_r, n_in_g, n_out_r, has_vres, tc):
    refs = list(refs)
    in_r, refs = refs[:n_in_r], refs[n_in_r:]
    in_g, refs = refs[:n_in_g], refs[n_in_g:]
    out_r, refs = refs[:n_out_r], refs[n_out_r:]
    out_g, refs = refs[:1], refs[1:]
    scr_r, scr_g = refs[:2], refs[2:]
    bodies = [_rwkv_body(in_r + out_r + scr_r, has_vres, tc), _gdn_body(in_g + out_g + scr_g, tc)]
    while bodies:
        for body in list(bodies):
            if next(body, bodies) is bodies:
                bodies.remove(body)


def _mixer(y_r, rparams, y_qkv, y_z, y_ba, gparams, seq, y_vres=None, v_first=None, vres_params=None, tc=512):
    t = y_r.shape[0]
    tc = min(tc, seq)
    per_b = seq // tc
    nb = t // seq
    has_vres = y_vres is not None
    w = RWKV_WIDTH
    row = lambda b, s: (b * per_b + s, 0)
    once = dict(index_map=lambda b, s: (0, 0), pipeline_mode=pl.Buffered(1))
    tile = lambda width: pl.BlockSpec((tc, width), row)

    in_r = [y_r] + ([y_vres, v_first] if has_vres else [])
    spec_r = [tile(a.shape[1]) for a in in_r]
    plist = list(rparams) + (list(vres_params) if has_vres else [])
    in_r += plist
    spec_r += [pl.BlockSpec(p.shape, **once) for p in plist]
    in_g = [y_qkv, y_z, y_ba] + list(gparams)
    spec_g = [tile(GDN_QKV), tile(GDN_WIDTH), tile(LANE)] + [pl.BlockSpec(p.shape, **once) for p in gparams]

    n_out_r = 1 if has_vres else 2
    n_out = n_out_r + 1
    scratch = [pltpu.VMEM((RWKV_WIDTH // MXU_TILE, MXU_TILE, MXU_TILE), F32), pltpu.VMEM((tc, w), F32),
               pltpu.VMEM((GDN_HEADS, GDN_HEAD_DIM, GDN_HEAD_DIM), F32), pltpu.VMEM((tc, GDN_WIDTH), F32)]
    return pl.pallas_call(
        functools.partial(_mixer_kernel, n_in_r=len(in_r), n_in_g=len(in_g), n_out_r=n_out_r,
                          has_vres=has_vres, tc=tc),
        grid=(nb, per_b),
        in_specs=spec_r + spec_g,
        out_specs=[tile(w)] * n_out,
        out_shape=[jax.ShapeDtypeStruct((t, w), F32)] * n_out,
        scratch_shapes=scratch,
        compiler_params=pltpu.CompilerParams(
            dimension_semantics=("parallel", "arbitrary"), vmem_limit_bytes=VMEM_LIMIT),
        name="mixer_vres" if has_vres else "mixer",
    )(*in_r, *in_g)


def _pad_cols(m, width):
    return jnp.pad(m, ((0, 0), (0, width - m.shape[1])))


def kernel(x, c, norm_gain, ada_w, ada_b, ffn_w_gu, ffn_w_down, w_in, w_out, rwkv_mu, rwkv_w0, rwkv_w_up, rwkv_a0, rwkv_a_up, rwkv_g_up, rwkv_k_k, rwkv_k_a, rwkv_r_k, rwkv_gn_w, rwkv_gn_b, vres_w_down, vres_mu, vres_w_up, vres_v0, gdn_conv_w, gdn_a_log, gdn_dt_bias, gdn_norm_w, final_gain):
    bsz, seq, d = x.shape
    depth = norm_gain.shape[0]
    nh = GDN_HEADS
    mod = _adaln(c, ada_w, ada_b)
    w_gu = ffn_w_gu.astype(BF16)
    w_dn = ffn_w_down.astype(BF16)
    xf = x.reshape(bsz * seq, d)
    v_first = None
    row = lambda p: p.reshape(1, -1)
    for l in range(depth):
        def mods(sub):
            m = mod[l * 3 + sub]
            return tuple(m[:, i * d:(i + 1) * d].reshape(bsz, 1, d) for i in range(3))

        shift, scale, gate = mods(0)
        xf = _ffn(xf, norm_gain[l, 0], shift, scale, gate, w_gu, w_dn, (l, 0), seq)

        shift, scale, gate = mods(1)
        wl = w_in[l]
        w_ba = _pad_cols(wl[:, RWKV_COLS + 4 * GDN_WIDTH:], LANE)
        weights = [wl[:, :RWKV_COLS], wl[:, RWKV_COLS:RWKV_COLS + GDN_QKV],
                   wl[:, RWKV_COLS + GDN_QKV:RWKV_COLS + 4 * GDN_WIDTH], w_ba]
        if l > 0:
            weights.append(_pad_cols(vres_w_down[l - 1], LANE))
        ys = _inproj(xf, norm_gain[l, 1], shift, scale, row(rwkv_mu[l]), gdn_conv_w[l],
                     [wt.astype(BF16) for wt in weights], seq,
                     vres_mu=_pad_cols(row(vres_mu[l - 1]), LANE) if l > 0 else None)
        y_r, y_qkv, y_z, y_ba = ys[:4]

        rparams = [row(rwkv_w0[l]), rwkv_w_up[l], row(rwkv_a0[l]), rwkv_a_up[l],
                   rwkv_g_up[l], row(rwkv_k_k[l]), row(rwkv_k_a[l]), row(rwkv_r_k[l]),
                   row(rwkv_gn_w[l]), row(rwkv_gn_b[l])]
        alog = gdn_a_log[l].astype(F32)
        dtb = gdn_dt_bias[l].astype(F32)
        place = lambda p: jnp.pad(p, (nh, LANE - 2 * nh))
        gparams = [row(place(alog)), row(place(dtb)),
                   place(alog)[:SUBLANE].reshape(SUBLANE, 1), place(dtb)[:SUBLANE].reshape(SUBLANE, 1),
                   row(gdn_norm_w[l])]
        if l == 0:
            o_r, v_first, o_g = _mixer(y_r, rparams, y_qkv, y_z, y_ba, gparams, seq)
        else:
            vparams = [jnp.pad(vres_w_up[l - 1], ((0, LANE - VRES_RANK), (0, 0))),
                       row(vres_v0[l - 1])]
            o_r, o_g = _mixer(y_r, rparams, y_qkv, y_z, y_ba, gparams, seq,
                              y_vres=ys[4], v_first=v_first, vres_params=vparams)

        wo = w_out[l].astype(BF16)
        mix = (gate, o_r, o_g, wo[:RWKV_WIDTH], wo[RWKV_WIDTH:])
        shift, scale, gate = mods(2)
        xf = _ffn(xf, norm_gain[l, 2], shift, scale, gate, w_gu, w_dn, (l, 1), seq,
                  final_gain=final_gain if l == depth - 1 else None, mix=mix)
    return xf.reshape(bsz, seq, d)
```

```python
import functools
import math

import jax
import jax.numpy as jnp
from jax import lax
from jax.experimental import pallas as pl
from jax.experimental.pallas import tpu as pltpu

F32 = jnp.float32
BF16 = jnp.bfloat16

RWKV_HEAD_DIM = 64
RWKV_HEADS = 8
RWKV_WIDTH = RWKV_HEAD_DIM * RWKV_HEADS
GDN_HEAD_DIM = 128
GDN_HEADS = 4
GDN_WIDTH = GDN_HEAD_DIM * GDN_HEADS
DECAY_RANK = 64
ICLR_RANK = 64
VRES_RANK = 32
GATE_RANK = 128
RWKV_COLS = 3 * RWKV_WIDTH + DECAY_RANK + ICLR_RANK + GATE_RANK
GDN_QKV = 3 * GDN_WIDTH
CONV_WIDTH = 4
CHUNK = 64
LANE = 128
SUBLANE = 8
MXU_TILE = 256
assert CHUNK == RWKV_HEAD_DIM and MXU_TILE % RWKV_HEAD_DIM == 0 and RWKV_WIDTH % MXU_TILE == 0
assert GDN_HEADS * CHUNK == MXU_TILE
EPS_RMS = 1e-6
EPS_GN = 64e-5
EPS_L2 = 1e-6
MACARON_WEIGHT = 0.5
VMEM_LIMIT = 52 * 1024 * 1024

NT = (((1,), (1,)), ((), ()))
TN = (((0,), (0,)), ((), ()))


def _sigmoid(x):
    return 1.0 / (1.0 + jnp.exp(-x))


def _softplus(x):
    return jnp.maximum(x, 0.0) + jnp.log(1.0 + jnp.exp(-jnp.abs(x)))


def _silu(x):
    return x * _sigmoid(x)


def _dot(a, b, precision=None):
    return jnp.dot(a, b, preferred_element_type=F32, precision=precision)


def _dg(a, b, dims, precision=None):
    return lax.dot_general(a, b, dims, preferred_element_type=F32, precision=precision)


def _bdg(a, b, dims):
    return lax.dot_general(a.astype(BF16), b.astype(BF16), dims, preferred_element_type=F32)


def _split2(a):
    hi = a.astype(BF16)
    return hi, (a - hi.astype(F32)).astype(BF16)


def _stack_bd(xb, mask, reps):
    return jnp.where(mask, jnp.concatenate([xb] * reps, axis=0), jnp.zeros((), BF16))


def _dot3_bd(a, y, stack):
    ah, al = _split2(a)
    yh, yl = _split2(y)
    bh, bl = stack(yh), stack(yl)
    d = lambda p, q: jnp.dot(p, q, preferred_element_type=F32)
    rows = a.shape[0]
    both = d(jnp.concatenate([ah, al], axis=0), bh)
    return both[:rows] + (both[rows:] + d(ah, bl))


def _split3(a):
    hi = a.astype(BF16)
    r1 = a - hi.astype(F32)
    mid = r1.astype(BF16)
    return hi, mid, (r1 - mid.astype(F32)).astype(BF16)


def _sel_dot(sel, x):
    sb = sel.astype(BF16)
    hi, mid, lo = _split3(x)
    d = lambda y: jnp.dot(sb, y, preferred_element_type=F32)
    return d(hi) + (d(mid) + d(lo))


def _dot_sel(x, sel):
    sb = sel.astype(BF16)
    hi, mid, lo = _split3(x)
    d = lambda y: jnp.dot(y, sb, preferred_element_type=F32)
    return d(hi) + (d(mid) + d(lo))


def _norm_mod(x, gain, shift, scale):
    ms = jnp.mean(x * x, axis=-1, keepdims=True)
    y = x * lax.rsqrt(ms + EPS_RMS) * gain
    return y * (1.0 + scale) + shift


def _tri(n, strict):
    r = lax.broadcasted_iota(jnp.int32, (n, n), 0)
    c = lax.broadcasted_iota(jnp.int32, (n, n), 1)
    return (r > c) if strict else (r >= c)


def _mod_kernel(c_ref, w_ref, b_ref, o_ref):
    ch, cl = _split2(_silu(c_ref[...]))
    wh, wl = _split2(w_ref[0])
    o_ref[0] = _dot(ch, wh) + (_dot(ch, wl) + _dot(cl, wh)) + b_ref[0]


def _adaln(c, ada_w, ada_b):
    n_l, n_sub, d, d3 = ada_w.shape
    b = c.shape[0]
    rows = -(-b // SUBLANE) * SUBLANE
    c_pad = jnp.pad(c, ((0, rows - b), (0, 0)))
    tn = d3
    out = pl.pallas_call(
        _mod_kernel,
        grid=(n_l * n_sub, d3 // tn),
        in_specs=[
            pl.BlockSpec((rows, d), lambda i, j: (0, 0)),
            pl.BlockSpec((1, d, tn), lambda i, j: (i, 0, j)),
            pl.BlockSpec((1, 1, tn), lambda i, j: (i, 0, j)),
        ],
        out_specs=pl.BlockSpec((1, rows, tn), lambda i, j: (i, 0, j)),
        out_shape=jax.ShapeDtypeStruct((n_l * n_sub, rows, d3), F32),
        compiler_params=pltpu.CompilerParams(
            dimension_semantics=("parallel", "parallel"), vmem_limit_bytes=VMEM_LIMIT),
        name="adaln_mod",
    )(c_pad, ada_w.reshape(n_l * n_sub, d, d3), ada_b.reshape(n_l * n_sub, 1, d3))
    return out[:, :b, :]


def _ffn_kernel(*refs, final, mix, sub):
    refs = list(refs)
    x_ref, gain_ref, sh_ref, sc_ref, gt_ref, wgu_ref, wd_ref = refs[:7]
    del refs[:7]
    if mix:
        gm_ref, or_ref, og_ref, w1_ref, w2_ref = refs[:5]
        del refs[:5]
    if final:
        fg_ref = refs.pop(0)
    (o_ref,) = refs

    x = x_ref[...]
    if mix:
        mixed = _dot(or_ref[...].astype(BF16), w1_ref[...]) + _dot(og_ref[...].astype(BF16), w2_ref[...])
        x = x + gm_ref[0] * mixed
    h = _norm_mod(x, gain_ref[...], sh_ref[0], sc_ref[0]).astype(BF16)

    f = wd_ref.shape[0]
    blocks = [(a, min(a + sub, f)) for a in range(0, f, sub)]
    gu = [(_dot(h, wgu_ref[:, a:b]), _dot(h, wgu_ref[:, f + a:f + b])) for a, b in blocks]
    down = None
    for (a, b), (g, u) in zip(blocks, gu):
        part = _dot((_silu(g) * u).astype(BF16), wd_ref[a:b, :])
        down = part if down is None else down + part

    y = x + (MACARON_WEIGHT * gt_ref[0]) * down
    if final:
        ms = jnp.mean(y * y, axis=-1, keepdims=True)
        y = y * lax.rsqrt(ms + EPS_RMS) * fg_ref[...]
    o_ref[...] = y


def _ffn(x, gain, shift, scale, gate, w_gu, w_dn, which, seq, final_gain=None, mix=None, tm=512,
         sub=3 * MXU_TILE):
    t, d = x.shape
    tm = min(tm, seq)
    per_b = seq // tm
    final = final_gain is not None
    row = lambda i: (i, 0)
    once = dict(index_map=lambda i: (0, 0), pipeline_mode=pl.Buffered(1))
    modspec = pl.BlockSpec((1, 1, d), lambda i: (i // per_b, 0, 0))
    in_specs = [
        pl.BlockSpec((tm, d), row),
        pl.BlockSpec((1, d), **once),
        modspec, modspec, modspec,
        pl.BlockSpec((None, None) + w_gu.shape[2:], lambda i: which + (0, 0), pipeline_mode=pl.Buffered(1)),
        pl.BlockSpec((None, None) + w_dn.shape[2:], lambda i: which + (0, 0), pipeline_mode=pl.Buffered(1)),
    ]
    args = [x, gain.reshape(1, d), shift, scale, gate, w_gu, w_dn]
    if mix is not None:
        gate_mix, o_r, o_g, w1, w2 = mix
        in_specs += [modspec, pl.BlockSpec((tm, o_r.shape[1]), row), pl.BlockSpec((tm, o_g.shape[1]), row),
                     pl.BlockSpec(w1.shape, **once), pl.BlockSpec(w2.shape, **once)]
        args += [gate_mix, o_r, o_g, w1, w2]
    if final:
        in_specs.append(pl.BlockSpec((1, d), **once))
        args.append(final_gain.reshape(1, d))
    return pl.pallas_call(
        functools.partial(_ffn_kernel, final=final, mix=mix is not None, sub=sub),
        grid=(t // tm,),
        in_specs=in_specs,
        out_specs=pl.BlockSpec((tm, d), row),
        out_shape=jax.ShapeDtypeStruct((t, d), F32),
        compiler_params=pltpu.CompilerParams(
            dimension_semantics=("parallel",), vmem_limit_bytes=VMEM_LIMIT),
        name="ffn" + ("_mix" if mix is not None else "") + ("_final" if final else ""),
    )(*args)


def _shift_rows(y, carry_row):
    rolled = pltpu.roll(y, 1, 0)
    first = lax.broadcasted_iota(jnp.int32, y.shape, 0) == 0
    return jnp.where(first, carry_row, rolled)


def _inproj_kernel(*refs, has_vres, per_b, tm, col):
    refs = list(refs)
    x_ref, gain_ref, sh_ref, sc_ref, mu_ref, cw_ref, win_ref, wb_ref = refs[:8]
    del refs[:8]
    if has_vres:
        vmu_ref, wv_ref = refs[:2]
        del refs[:2]
    or_ref, oq_ref, oz_ref, ob_ref = refs[:4]
    del refs[:4]
    if has_vres:
        ov_ref = refs.pop(0)
    carry_r, ext = refs[:2]
    carry_v = refs[2] if has_vres else None
    pad = SUBLANE

    @pl.when(pl.program_id(0) % per_b == 0)
    def _():
        carry_r[...] = jnp.zeros_like(carry_r)
        ext[0:pad, :] = jnp.zeros((pad, GDN_QKV), F32)
        if has_vres:
            carry_v[...] = jnp.zeros_like(carry_v)

    h = _norm_mod(x_ref[...], gain_ref[...], sh_ref[0], sc_ref[0]).astype(BF16)
    blocks = lambda width: [(a, min(a + col, width)) for a in range(0, width, col)]
    for a, b in blocks(GDN_QKV):
        q = _dot(h, win_ref[:, RWKV_COLS + a:RWKV_COLS + b])
        ext[pad:pad + tm, a:b] = q
        conv = q * cw_ref[CONV_WIDTH - 1:CONV_WIDTH, a:b]
        for j in range(CONV_WIDTH - 1):
            back = CONV_WIDTH - 1 - j
            conv = conv + ext[pad - back:pad - back + tm, a:b] * cw_ref[j:j + 1, a:b]
        ext[0:pad, a:b] = q[tm - pad:tm, :]
        oq_ref[:, a:b] = _silu(conv)

    for a, b in blocks(RWKV_COLS):
        y = _dot(h, win_ref[:, a:b])
        prev = _shift_rows(y, carry_r[0:1, a:b])
        carry_r[0:1, a:b] = y[tm - 1:tm, :]
        or_ref[:, a:b] = y + (prev - y) * mu_ref[:, a:b]
    if has_vres:
        yv = _dot(h, wv_ref[...])
        prev_v = _shift_rows(yv, carry_v[0:1, :])
        carry_v[0:1, :] = yv[tm - 1:tm, :]
        ov_ref[...] = yv + (prev_v - yv) * vmu_ref[...]

    oz_ref[...] = _dot(h, win_ref[:, RWKV_COLS + GDN_QKV:RWKV_COLS + GDN_QKV + GDN_WIDTH])
    ob_ref[...] = _dot(h, wb_ref[...])


def _inproj(x, gain, shift, scale, mu, conv_w, w_in, layer, w_ba, seq, vres_mu=None, w_vres=None, tm=512,
            col=2 * MXU_TILE):
    t, d = x.shape
    tm = min(tm, seq)
    per_b = seq // tm
    has_vres = vres_mu is not None
    row = lambda i: (i, 0)
    once = dict(index_map=lambda i: (0, 0), pipeline_mode=pl.Buffered(1))
    modspec = pl.BlockSpec((1, 1, d), lambda i: (i // per_b, 0, 0))
    small = [gain.reshape(1, d), shift, scale, mu, conv_w]
    in_specs = [pl.BlockSpec((tm, d), row), pl.BlockSpec((1, d), **once), modspec, modspec,
                pl.BlockSpec(mu.shape, **once), pl.BlockSpec(conv_w.shape, **once)]
    in_specs += [pl.BlockSpec((None,) + w_in.shape[1:], lambda i: (layer, 0, 0), pipeline_mode=pl.Buffered(1)),
                 pl.BlockSpec(w_ba.shape, **once)]
    args = [x] + small + [w_in, w_ba]
    widths = [RWKV_COLS, GDN_QKV, GDN_WIDTH, w_ba.shape[1]]
    scratch = [pltpu.VMEM((SUBLANE, RWKV_COLS), F32), pltpu.VMEM((tm + SUBLANE, GDN_QKV), F32)]
    if has_vres:
        in_specs += [pl.BlockSpec(vres_mu.shape, **once), pl.BlockSpec(w_vres.shape, **once)]
        args += [vres_mu, w_vres]
        widths.append(w_vres.shape[1])
        scratch.append(pltpu.VMEM((SUBLANE, LANE), F32))
    return pl.pallas_call(
        functools.partial(_inproj_kernel, has_vres=has_vres, per_b=per_b, tm=tm, col=col),
        grid=(t // tm,),
        in_specs=in_specs,
        out_specs=[pl.BlockSpec((tm, n), row) for n in widths],
        out_shape=[jax.ShapeDtypeStruct((t, n), F32) for n in widths],
        scratch_shapes=scratch,
        compiler_params=pltpu.CompilerParams(
            dimension_semantics=("arbitrary",), vmem_limit_bytes=VMEM_LIMIT),
        name="inproj",
    )(*args)


def _rwkv_body(refs, has_vres, tc):
    if has_vres:
        (y_ref, yv_ref, vf_ref, w0_ref, wup_ref, a0_ref, aup_ref, gup_ref, kk_ref, ka_ref,
         rk_ref, gnw_ref, gnb_ref, vup_ref, v0_ref,
         o_ref, state, o_scr) = refs
    else:
        (y_ref, w0_ref, wup_ref, a0_ref, aup_ref, gup_ref, kk_ref, ka_ref,
         rk_ref, gnw_ref, gnb_ref,
         o_ref, v_ref, state, o_scr) = refs
    n, w = RWKV_HEAD_DIM, RWKV_WIDTH

    @pl.when(pl.program_id(1) == 0)
    def _():
        state[...] = jnp.zeros_like(state)

    yield
    ym = y_ref[...]
    r = ym[:, 0:w]
    k = ym[:, w:2 * w]
    v = ym[:, 2 * w:3 * w]
    wd = ym[:, 3 * w:3 * w + DECAY_RANK]
    ad = ym[:, 3 * w + DECAY_RANK:3 * w + DECAY_RANK + ICLR_RANK]
    gd = ym[:, 3 * w + DECAY_RANK + ICLR_RANK:]

    if has_vres:
        logit = v0_ref[...] + _dot(yv_ref[...].astype(BF16), vup_ref[...].astype(BF16))
        v = v + (vf_ref[...] - v) * _sigmoid(logit)
    else:
        v_ref[...] = v

    z = w0_ref[...] + _dot(jnp.tanh(wd).astype(BF16), wup_ref[...].astype(BF16))
    logw = -math.exp(-0.5) * _sigmoid(z)
    a = _sigmoid(a0_ref[...] + _dot(ad.astype(BF16), aup_ref[...].astype(BF16)))
    g = _dot(_sigmoid(gd).astype(BF16), gup_ref[...].astype(BF16))

    gw = MXU_TILE
    hpg = gw // n
    groups = [slice(gi * gw, (gi + 1) * gw) for gi in range(w // gw)]
    rr = lax.broadcasted_iota(jnp.int32, (gw, gw), 0)
    cc = lax.broadcasted_iota(jnp.int32, (gw, gw), 1)
    same_blk = (rr // CHUNK) == (cc // CHUNK)
    head_ones = (rr // n) == (cc // n)
    tt = lax.broadcasted_iota(jnp.int32, (CHUNK, gw), 0)
    ss = lax.broadcasted_iota(jnp.int32, (CHUNK, gw), 1) % CHUNK
    ls_strict = tt > ss
    ls_incl = tt >= ss
    eye_ls = (tt == ss).astype(F32)
    l_incl = _tri(CHUNK, False)

    def head_sum(x):
        hi, lo = _split2(x)
        rows = x.shape[0]
        parts = [p[:, ln] for p in (hi, lo) for ln in groups]
        r = jnp.dot(jnp.concatenate(parts, axis=0), head_ones.astype(BF16), preferred_element_type=F32)
        ng = len(groups)
        return jnp.concatenate([r[gi * rows:(gi + 1) * rows] + r[(ng + gi) * rows:(ng + gi + 1) * rows]
                                for gi in range(ng)], axis=1)

    def bd(x):
        return _stack_bd(x.astype(BF16), same_blk, hpg)

    def ls_dot(a, b_bd):
        return jnp.dot(a.astype(BF16), b_bd, preferred_element_type=F32)

    yield
    kkr = k * kk_ref[...]
    kk = kkr * lax.rsqrt(head_sum(kkr * kkr) + EPS_L2)
    k2 = k * (1.0 + (a - 1.0) * ka_ref[...])
    b = kk * a
    av = -kk

    n_chunks = tc // CHUNK
    insts = [(c, gi) for c in range(n_chunks) for gi in range(len(groups))]
    d = {}
    for c in range(n_chunks):
        rows = slice(c * CHUNK, (c + 1) * CHUNK)
        lw = logw[rows]
        cum = _sel_dot(l_incl, lw)
        last = cum[CHUNK - 1:CHUNK, :]
        e_pos = jnp.exp(cum)
        e_neg = jnp.exp(-cum)
        e_to_end = jnp.exp(last - cum)
        gam_end = jnp.exp(last)
        at = av[rows] * jnp.exp(cum - lw)
        rt = r[rows] * e_pos
        bt = b[rows] * e_neg
        kt = k2[rows] * e_neg
        bg = b[rows] * e_to_end
        kg = k2[rows] * e_to_end
        vc = v[rows]
        for gi, ln in enumerate(groups):
            d[c, gi] = dict(at=at[:, ln], rt=rt[:, ln], bt=bt[:, ln], kt=kt[:, ln], bg=bg[:, ln],
                            kg=kg[:, ln], v=vc[:, ln], gam=gam_end[:, ln])
    yield
    for i in insts:
        e = d[i]
        x_ar = jnp.concatenate([e['at'], e['rt']], axis=0).astype(BF16)
        y_bk = jnp.concatenate([bd(e['bt']), bd(e['kt'])], axis=0)
        m1 = _dg(x_ar, y_bk, NT)
        e['n'] = jnp.where(ls_strict, m1[:CHUNK, :gw], 0.0)
        e['a_ak'] = jnp.where(ls_strict, m1[:CHUNK, gw:], 0.0)
        e['a_rb'] = jnp.where(ls_incl, m1[CHUNK:, :gw], 0.0)
        e['a_rk'] = jnp.where(ls_incl, m1[CHUNK:, gw:], 0.0)
        e['t'] = eye_ls + e['n']
        e['v_bd'] = bd(e['v'])
    yield
    for i in insts:
        d[i]['n'] = ls_dot(d[i]['n'], bd(d[i]['n']))
    yield
    steps = CHUNK.bit_length() - 2
    for k in range(steps):
        for i in insts:
            e = d[i]
            if k + 1 < steps:
                both = ls_dot(jnp.concatenate([e['t'], e['n']], axis=0), bd(e['n']))
                e['t'] = e['t'] + both[:CHUNK]
                e['n'] = both[CHUNK:]
            else:
                e['t'] = e['t'] + ls_dot(e['t'], bd(e['n']))
        yield
    for i in insts:
        e = d[i]
        e['w'] = ls_dot(e['t'], bd(e['at']))
        both = ls_dot(jnp.concatenate([e['a_ak'], e['a_rk']], axis=0), e['v_bd'])
        e['akv'] = both[:CHUNK]
        e['arkv'] = both[CHUNK:]
    yield
    for i in insts:
        e = d[i]
        e['u0'] = ls_dot(e['t'], bd(e['akv']))
        e['q'] = e['rt'] + ls_dot(e['a_rb'], bd(e['w']))
    yield
    for i in insts:
        e = d[i]
        e['o0'] = ls_dot(e['a_rb'], bd(e['u0'])) + e['arkv']
        e['m'] = jnp.where(same_blk, _bdg(e['w'], e['bg'], TN), 0.0)
        uv = jnp.concatenate([e['u0'], e['v']], axis=0)
        bk_end = jnp.concatenate([e['bg'], e['kg']], axis=0)
        e['n0'] = jnp.where(same_blk, _bdg(uv, bk_end, TN), 0.0)
    yield
    for c, gi in insts:
        e = d[c, gi]
        s0 = state[gi]
        sb = s0.astype(BF16)
        o_scr[c * CHUNK:(c + 1) * CHUNK, groups[gi]] = _dg(e['q'].astype(BF16), sb, NT) + e['o0']
        state[gi] = s0 * e['gam'] + jnp.dot(sb, e['m'].astype(BF16), preferred_element_type=F32) + e['n0']
    yield

    o = o_scr[...]
    inv_n = 1.0 / n
    mean = head_sum(o) * inv_n
    dev = o - mean
    var = head_sum(dev * dev) * inv_n
    on = dev * lax.rsqrt(var + EPS_GN) * gnw_ref[...] + gnb_ref[...]
    bonus = head_sum(r * k2 * rk_ref[...])
    o_ref[...] = (on + bonus * v) * g


def _gdn_body(refs, tc):
    (qkv_ref, z_ref, ba_ref, alog_r_ref, dt_r_ref, alog_c_ref, dt_c_ref, nw_ref,
     o_ref, state, o_scr) = refs
    dh, nh, w = GDN_HEAD_DIM, GDN_HEADS, GDN_WIDTH
    pad = SUBLANE

    @pl.when(pl.program_id(1) == 0)
    def _():
        state[...] = jnp.zeros_like(state)

    yield
    qkv = qkv_ref[...]

    ba = ba_ref[...]
    beta_c = _sigmoid(ba)
    g_c = -jnp.exp(alog_r_ref[...]) * _softplus(ba + dt_r_ref[...])
    er = lax.broadcasted_iota(jnp.int32, (pad, LANE), 0)
    ec = lax.broadcasted_iota(jnp.int32, (pad, LANE), 1)
    pick = (er == ec).astype(BF16)
    ba_hi, ba_mid, ba_lo = _split3(ba)
    ba_t = _dg(pick, ba_hi, NT) + (_dg(pick, ba_mid, NT) + _dg(pick, ba_lo, NT))
    g_r = -jnp.exp(alog_c_ref[...]) * _softplus(ba_t + dt_c_ref[...])

    yield
    gw = nh * CHUNK
    rr = lax.broadcasted_iota(jnp.int32, (gw, gw), 0)
    cc = lax.broadcasted_iota(jnp.int32, (gw, gw), 1)
    same_blk = (rr // CHUNK) == (cc // CHUNK)
    u_bd = jnp.logical_and(same_blk, rr <= cc)
    tt = lax.broadcasted_iota(jnp.int32, (CHUNK, gw), 0)
    ss = lax.broadcasted_iota(jnp.int32, (CHUNK, gw), 1) % CHUNK
    ls_strict = tt > ss
    ls_incl = tt >= ss
    eye_ls = (tt == ss).astype(F32)
    l_incl = _tri(CHUNK, False)
    src = lax.broadcasted_iota(jnp.int32, (LANE, gw), 0)
    spread_g = src == nh + lax.broadcasted_iota(jnp.int32, (LANE, gw), 1) // CHUNK
    spread_b = src == lax.broadcasted_iota(jnp.int32, (LANE, gw), 1) // CHUNK
    srcw = lax.broadcasted_iota(jnp.int32, (LANE, w), 0)
    spread_gw = srcw == nh + lax.broadcasted_iota(jnp.int32, (LANE, w), 1) // dh
    spread_bw = srcw == lax.broadcasted_iota(jnp.int32, (LANE, w), 1) // dh
    head_rows = (lax.broadcasted_iota(jnp.int32, (gw, w), 0) // CHUNK
                 == lax.broadcasted_iota(jnp.int32, (gw, w), 1) // dh)

    def bd(xb):
        return _stack_bd(xb, same_blk, nh)

    def bd_wide(xb):
        return _stack_bd(xb, head_rows, nh)

    def ls_dot(a, y):
        return jnp.dot(a.astype(BF16), bd(y.astype(BF16)), preferred_element_type=F32)

    def l2norm_heads(x):
        parts = []
        for h in range(nh):
            xh = x[:, h * dh:(h + 1) * dh]
            parts.append(xh * lax.rsqrt(jnp.sum(xh * xh, axis=-1, keepdims=True) + EPS_L2))
        return jnp.concatenate(parts, axis=1)

    n_chunks = tc // CHUNK
    heads = [slice(h * dh, (h + 1) * dh) for h in range(nh)]
    d = []
    for c in range(n_chunks):
        rows = slice(c * CHUNK, (c + 1) * CHUNK)
        g_rows = g_c[rows]
        b_rows = beta_c[rows]
        cum_c = _sel_dot(l_incl, g_rows)
        acum = _dot_sel(cum_c, spread_g)
        cumw = _dot_sel(cum_c, spread_gw)
        beta = _dot_sel(b_rows, spread_b)
        betaw = _dot_sel(b_rows, spread_bw)
        g_row = jnp.concatenate([g_r[nh + h:nh + h + 1, rows] for h in range(nh)], axis=1)
        crow = _dot_sel(jnp.broadcast_to(g_row, (SUBLANE, gw)), u_bd)[0:1, :]
        decay = jnp.where(ls_incl, jnp.exp(jnp.where(ls_incl, acum - crow, 0.0)), 0.0)
        qc = l2norm_heads(qkv[rows, 0:w]) * (dh ** -0.5)
        kc = l2norm_heads(qkv[rows, w:2 * w])
        vc = qkv[rows, 2 * w:3 * w]
        kq = jnp.concatenate([kc, qc], axis=0).astype(BF16)
        m = _dg(kq, bd_wide(kc.astype(BF16)), NT)
        n = -jnp.where(ls_strict, m[:CHUNK] * decay * beta, 0.0)
        d.append(dict(n1=n, n=n, t=eye_ls + n, p=m[CHUNK:] * decay, cumw=cumw, betaw=betaw, qc=qc, kc=kc, vc=vc))
    yield
    for e in d:
        e['n'] = ls_dot(e['n'], e['n'])
    yield
    steps = CHUNK.bit_length() - 2
    for k in range(steps):
        for e in d:
            if k + 1 < steps:
                both = ls_dot(jnp.concatenate([e['t'], e['n']], axis=0), e['n'])
                e['t'] = e['t'] + both[:CHUNK]
                e['n'] = both[CHUNK:]
            else:
                e['t'] = e['t'] + ls_dot(e['t'], e['n'])
        yield
    for e in d:
        e['r'] = eye_ls - e['t'] + _dot3_bd(e['n1'], e['t'], bd)
    yield
    for e in d:
        e['t'] = e['t'] + ls_dot(e['t'], e['r'])
    yield
    for e in d:
        e_c = jnp.exp(e['cumw'])
        clast = e['cumw'][CHUNK - 1:CHUNK, :]
        tb = e['t'].astype(BF16)
        e['u'] = _dot(tb, bd_wide((e['vc'] * e['betaw']).astype(BF16)))
        e['wm'] = _dot(tb, bd_wide((e['kc'] * (e['betaw'] * e_c)).astype(BF16)))
        e['q_dec'] = e['qc'] * e_c
        e['k_dec'] = e['kc'] * jnp.exp(clast - e['cumw'])
        e['g'] = jnp.exp(clast)
    yield
    for e in d:
        pb = e['p'].astype(BF16)
        e['qe'] = e['q_dec'] - _dot(pb, bd_wide(e['wm'].astype(BF16)))
        e['o0'] = _dot(pb, bd_wide(e['u'].astype(BF16)))
        e['kw'] = [_bdg(e['k_dec'][:, ln], e['wm'][:, ln], TN) for ln in heads]
        e['n0'] = [_bdg(e['k_dec'][:, ln], e['u'][:, ln], TN) for ln in heads]
    yield
    for c, e in enumerate(d):
        outs = []
        for h, ln in enumerate(heads):
            s0 = state[h]
            lhs = jnp.concatenate([e['qe'][:, ln], e['kw'][h]], axis=0).astype(BF16)
            both = _dot(lhs, s0.astype(BF16))
            outs.append(both[:CHUNK] + e['o0'][:, ln])
            state[h] = s0 * e['g'][:, ln] - both[CHUNK:] + e['n0'][h]
        o_scr[c * CHUNK:(c + 1) * CHUNK, :] = jnp.concatenate(outs, axis=1)
    yield

    zs = _silu(z_ref[...])
    for h in range(nh):
        ln = slice(h * dh, (h + 1) * dh)
        o = o_scr[:, ln]
        o = o * lax.rsqrt(jnp.mean(o * o, axis=-1, keepdims=True) + EPS_RMS) * nw_ref[...]
        o_ref[:, ln] = o * zs[:, ln]


def _mixer_kernel(*refs, n_in_r, n_in_g, n_out_r, has_vres, tc):
    refs = list(refs)
    in_r, refs = refs[:n_in_r], refs[n_in_r:]
    in_g, refs = refs[:n_in_g], refs[n_in_g:]
    out_r, refs = refs[:n_out_r], refs[n_out_r:]
    out_g, refs = refs[:1], refs[1:]
    scr_r, scr_g = refs[:2], refs[2:]
    bodies = [_rwkv_body(in_r + out_r + scr_r, has_vres, tc), _gdn_body(in_g + out_g + scr_g, tc)]
    while bodies:
        for body in list(bodies):
            if next(body, bodies) is bodies:
                bodies.remove(body)


def _mixer(y_r, rparams, y_qkv, y_z, y_ba, gparams, seq, y_vres=None, v_first=None, vres_params=None, tc=512):
    t = y_r.shape[0]
    tc = min(tc, seq)
    per_b = seq // tc
    nb = t // seq
    has_vres = y_vres is not None
    w = RWKV_WIDTH
    row = lambda b, s: (b * per_b + s, 0)
    once = dict(index_map=lambda b, s: (0, 0), pipeline_mode=pl.Buffered(1))
    tile = lambda width: pl.BlockSpec((tc, width), row)

    in_r = [y_r] + ([y_vres, v_first] if has_vres else [])
    spec_r = [tile(a.shape[1]) for a in in_r]
    plist = list(rparams) + (list(vres_params) if has_vres else [])
    in_r += plist
    spec_r += [pl.BlockSpec(p.shape, **once) for p in plist]
    in_g = [y_qkv, y_z, y_ba] + list(gparams)
    spec_g = [tile(GDN_QKV), tile(GDN_WIDTH), tile(LANE)] + [pl.BlockSpec(p.shape, **once) for p in gparams]

    n_out_r = 1 if has_vres else 2
    n_out = n_out_r + 1
    scratch = [pltpu.VMEM((RWKV_WIDTH // MXU_TILE, MXU_TILE, MXU_TILE), F32), pltpu.VMEM((tc, w), F32),
               pltpu.VMEM((GDN_HEADS, GDN_HEAD_DIM, GDN_HEAD_DIM), F32), pltpu.VMEM((tc, GDN_WIDTH), F32)]
    return pl.pallas_call(
        functools.partial(_mixer_kernel, n_in_r=len(in_r), n_in_g=len(in_g), n_out_r=n_out_r,
                          has_vres=has_vres, tc=tc),
        grid=(nb, per_b),
        in_specs=spec_r + spec_g,
        out_specs=[tile(w)] * n_out,
        out_shape=[jax.ShapeDtypeStruct((t, w), F32)] * n_out,
        scratch_shapes=scratch,
        compiler_params=pltpu.CompilerParams(
            dimension_semantics=("parallel", "arbitrary"), vmem_limit_bytes=VMEM_LIMIT),
        name="mixer_vres" if has_vres else "mixer",
    )(*in_r, *in_g)


def _pad_cols(m, width):
    return jnp.pad(m, ((0, 0), (0, width - m.shape[1])))


def kernel(x, c, norm_gain, ada_w, ada_b, ffn_w_gu, ffn_w_down, w_in, w_out, rwkv_mu, rwkv_w0, rwkv_w_up, rwkv_a0, rwkv_a_up, rwkv_g_up, rwkv_k_k, rwkv_k_a, rwkv_r_k, rwkv_gn_w, rwkv_gn_b, vres_w_down, vres_mu, vres_w_up, vres_v0, gdn_conv_w, gdn_a_log, gdn_dt_bias, gdn_norm_w, final_gain):
    bsz, seq, d = x.shape
    depth = norm_gain.shape[0]
    nh = GDN_HEADS
    mod = _adaln(c, ada_w, ada_b)
    w_gu = ffn_w_gu.astype(BF16)
    w_dn = ffn_w_down.astype(BF16)
    w_in_b = w_in.astype(BF16)
    xf = x.reshape(bsz * seq, d)
    v_first = None
    row = lambda p: p.reshape(1, -1)
    for l in range(depth):
        def mods(sub):
            m = mod[l * 3 + sub]
            return tuple(m[:, i * d:(i + 1) * d].reshape(bsz, 1, d) for i in range(3))

        shift, scale, gate = mods(0)
        xf = _ffn(xf, norm_gain[l, 0], shift, scale, gate, w_gu, w_dn, (l, 0), seq)

        shift, scale, gate = mods(1)
        w_ba = _pad_cols(w_in[l][:, RWKV_COLS + 4 * GDN_WIDTH:], LANE).astype(BF16)
        ys = _inproj(xf, norm_gain[l, 1], shift, scale, row(rwkv_mu[l]), gdn_conv_w[l], w_in_b, l, w_ba, seq,
                     vres_mu=_pad_cols(row(vres_mu[l - 1]), LANE) if l > 0 else None,
                     w_vres=_pad_cols(vres_w_down[l - 1], LANE).astype(BF16) if l > 0 else None)
        y_r, y_qkv, y_z, y_ba = ys[:4]

        rparams = [row(rwkv_w0[l]), rwkv_w_up[l], row(rwkv_a0[l]), rwkv_a_up[l],
                   rwkv_g_up[l], row(rwkv_k_k[l]), row(rwkv_k_a[l]), row(rwkv_r_k[l]),
                   row(rwkv_gn_w[l]), row(rwkv_gn_b[l])]
        alog = gdn_a_log[l].astype(F32)
        dtb = gdn_dt_bias[l].astype(F32)
        place = lambda p: jnp.pad(p, (nh, LANE - 2 * nh))
        gparams = [row(place(alog)), row(place(dtb)),
                   place(alog)[:SUBLANE].reshape(SUBLANE, 1), place(dtb)[:SUBLANE].reshape(SUBLANE, 1),
                   row(gdn_norm_w[l])]
        if l == 0:
            o_r, v_first, o_g = _mixer(y_r, rparams, y_qkv, y_z, y_ba, gparams, seq)
        else:
            vparams = [jnp.pad(vres_w_up[l - 1], ((0, LANE - VRES_RANK), (0, 0))),
                       row(vres_v0[l - 1])]
            o_r, o_g = _mixer(y_r, rparams, y_qkv, y_z, y_ba, gparams, seq,
                              y_vres=ys[4], v_first=v_first, vres_params=vparams)

        wo = w_out[l].astype(BF16)
        mix = (gate, o_r, o_g, wo[:RWKV_WIDTH], wo[RWKV_WIDTH:])
        shift, scale, gate = mods(2)
        xf = _ffn(xf, norm_gain[l, 2], shift, scale, gate, w_gu, w_dn, (l, 1), seq,
                  final_gain=final_gain if l == depth - 1 else None, mix=mix)
    return xf.reshape(bsz, seq, d)
```

```python
import functools
import math

import jax
import jax.numpy as jnp
from jax import lax
from jax.experimental import pallas as pl
from jax.experimental.pallas import tpu as pltpu

F32 = jnp.float32
BF16 = jnp.bfloat16

RWKV_HEAD_DIM = 64
RWKV_HEADS = 8
RWKV_WIDTH = RWKV_HEAD_DIM * RWKV_HEADS
GDN_HEAD_DIM = 128
GDN_HEADS = 4
GDN_WIDTH = GDN_HEAD_DIM * GDN_HEADS
DECAY_RANK = 64
ICLR_RANK = 64
VRES_RANK = 32
GATE_RANK = 128
RWKV_COLS = 3 * RWKV_WIDTH + DECAY_RANK + ICLR_RANK + GATE_RANK
GDN_QKV = 3 * GDN_WIDTH
CONV_WIDTH = 4
CHUNK = 64
LANE = 128
SUBLANE = 8
MXU_TILE = 256
assert CHUNK == RWKV_HEAD_DIM and MXU_TILE % RWKV_HEAD_DIM == 0 and RWKV_WIDTH % MXU_TILE == 0
assert GDN_HEADS * CHUNK == MXU_TILE
EPS_RMS = 1e-6
EPS_GN = 64e-5
EPS_L2 = 1e-6
MACARON_WEIGHT = 0.5
VMEM_LIMIT = 52 * 1024 * 1024

NT = (((1,), (1,)), ((), ()))
TN = (((0,), (0,)), ((), ()))


def _sigmoid(x):
    return 1.0 / (1.0 + jnp.exp(-x))


def _softplus(x):
    return jnp.maximum(x, 0.0) + jnp.log(1.0 + jnp.exp(-jnp.abs(x)))


def _silu(x):
    return x * _sigmoid(x)


def _dot(a, b, precision=None):
    return jnp.dot(a, b, preferred_element_type=F32, precision=precision)


def _dg(a, b, dims, precision=None):
    return lax.dot_general(a, b, dims, preferred_element_type=F32, precision=precision)


def _bdg(a, b, dims):
    return lax.dot_general(a.astype(BF16), b.astype(BF16), dims, preferred_element_type=F32)


def _split2(a):
    hi = a.astype(BF16)
    return hi, (a - hi.astype(F32)).astype(BF16)


def _stack_bd(xb, mask, reps):
    return jnp.where(mask, jnp.concatenate([xb] * reps, axis=0), jnp.zeros((), BF16))


def _dot3_bd(a, y, stack):
    ah, al = _split2(a)
    yh, yl = _split2(y)
    bh, bl = stack(yh), stack(yl)
    d = lambda p, q: jnp.dot(p, q, preferred_element_type=F32)
    rows = a.shape[0]
    both = d(jnp.concatenate([ah, al], axis=0), bh)
    return both[:rows] + (both[rows:] + d(ah, bl))


def _split3(a):
    hi = a.astype(BF16)
    r1 = a - hi.astype(F32)
    mid = r1.astype(BF16)
    return hi, mid, (r1 - mid.astype(F32)).astype(BF16)


def _sel_dot(sel, x):
    sb = sel.astype(BF16)
    hi, mid, lo = _split3(x)
    d = lambda y: jnp.dot(sb, y, preferred_element_type=F32)
    return d(hi) + (d(mid) + d(lo))


def _dot_sel(x, sel):
    sb = sel.astype(BF16)
    hi, mid, lo = _split3(x)
    d = lambda y: jnp.dot(y, sb, preferred_element_type=F32)
    return d(hi) + (d(mid) + d(lo))


def _norm_mod(x, gain, shift, scale):
    ms = jnp.mean(x * x, axis=-1, keepdims=True)
    y = x * lax.rsqrt(ms + EPS_RMS) * gain
    return y * (1.0 + scale) + shift


def _tri(n, strict):
    r = lax.broadcasted_iota(jnp.int32, (n, n), 0)
    c = lax.broadcasted_iota(jnp.int32, (n, n), 1)
    return (r > c) if strict else (r >= c)


def _mod_kernel(c_ref, w_ref, b_ref, o_ref):
    ch, cl = _split2(_silu(c_ref[...]))
    wh, wl = _split2(w_ref[0])
    o_ref[0] = _dot(ch, wh) + (_dot(ch, wl) + _dot(cl, wh)) + b_ref[0]


def _adaln(c, ada_w, ada_b):
    n_l, n_sub, d, d3 = ada_w.shape
    b = c.shape[0]
    rows = -(-b // SUBLANE) * SUBLANE
    c_pad = jnp.pad(c, ((0, rows - b), (0, 0)))
    tn = d3
    out = pl.pallas_call(
        _mod_kernel,
        grid=(n_l * n_sub, d3 // tn),
        in_specs=[
            pl.BlockSpec((rows, d), lambda i, j: (0, 0)),
            pl.BlockSpec((1, d, tn), lambda i, j: (i, 0, j)),
            pl.BlockSpec((1, 1, tn), lambda i, j: (i, 0, j)),
        ],
        out_specs=pl.BlockSpec((1, rows, tn), lambda i, j: (i, 0, j)),
        out_shape=jax.ShapeDtypeStruct((n_l * n_sub, rows, d3), F32),
        compiler_params=pltpu.CompilerParams(
            dimension_semantics=("parallel", "parallel"), vmem_limit_bytes=VMEM_LIMIT),
        name="adaln_mod",
    )(c_pad, ada_w.reshape(n_l * n_sub, d, d3), ada_b.reshape(n_l * n_sub, 1, d3))
    return out[:, :b, :]


def _ffn_kernel(*refs, final, mix, sub):
    refs = list(refs)
    x_ref, gain_ref, sh_ref, sc_ref, gt_ref, wgu_ref, wd_ref = refs[:7]
    del refs[:7]
    if mix:
        gm_ref, or_ref, og_ref, w1_ref, w2_ref = refs[:5]
        del refs[:5]
    if final:
        fg_ref = refs.pop(0)
    (o_ref,) = refs

    x = x_ref[...]
    if mix:
        mixed = _dot(or_ref[...].astype(BF16), w1_ref[...]) + _dot(og_ref[...].astype(BF16), w2_ref[...])
        x = x + gm_ref[0] * mixed
    h = _norm_mod(x, gain_ref[...], sh_ref[0], sc_ref[0]).astype(BF16)

    f = wd_ref.shape[0]
    blocks = [(a, min(a + sub, f)) for a in range(0, f, sub)]
    gu = [(_dot(h, wgu_ref[:, a:b]), _dot(h, wgu_ref[:, f + a:f + b])) for a, b in blocks]
    down = None
    for (a, b), (g, u) in zip(blocks, gu):
        part = _dot((_silu(g) * u).astype(BF16), wd_ref[a:b, :])
        down = part if down is None else down + part

    y = x + (MACARON_WEIGHT * gt_ref[0]) * down
    if final:
        ms = jnp.mean(y * y, axis=-1, keepdims=True)
        y = y * lax.rsqrt(ms + EPS_RMS) * fg_ref[...]
    o_ref[...] = y


def _ffn(x, gain, shift, scale, gate, w_gu, w_dn, which, seq, final_gain=None, mix=None, tm=512,
         sub=3 * MXU_TILE):
    t, d = x.shape
    tm = min(tm, seq)
    per_b = seq // tm
    final = final_gain is not None
    row = lambda i: (i, 0)
    once = dict(index_map=lambda i: (0, 0), pipeline_mode=pl.Buffered(1))
    modspec = pl.BlockSpec((1, 1, d), lambda i: (i // per_b, 0, 0))
    in_specs = [
        pl.BlockSpec((tm, d), row),
        pl.BlockSpec((1, d), **once),
        modspec, modspec, modspec,
        pl.BlockSpec((None, None) + w_gu.shape[2:], lambda i: which + (0, 0), pipeline_mode=pl.Buffered(1)),
        pl.BlockSpec((None, None) + w_dn.shape[2:], lambda i: which + (0, 0), pipeline_mode=pl.Buffered(1)),
    ]
    args = [x, gain.reshape(1, d), shift, scale, gate, w_gu, w_dn]
    if mix is not None:
        gate_mix, o_r, o_g, w1, w2 = mix
        in_specs += [modspec, pl.BlockSpec((tm, o_r.shape[1]), row), pl.BlockSpec((tm, o_g.shape[1]), row),
                     pl.BlockSpec(w1.shape, **once), pl.BlockSpec(w2.shape, **once)]
        args += [gate_mix, o_r, o_g, w1, w2]
    if final:
        in_specs.append(pl.BlockSpec((1, d), **once))
        args.append(final_gain.reshape(1, d))
    return pl.pallas_call(
        functools.partial(_ffn_kernel, final=final, mix=mix is not None, sub=sub),
        grid=(t // tm,),
        in_specs=in_specs,
        out_specs=pl.BlockSpec((tm, d), row),
        out_shape=jax.ShapeDtypeStruct((t, d), F32),
        compiler_params=pltpu.CompilerParams(
            dimension_semantics=("parallel",), vmem_limit_bytes=VMEM_LIMIT),
        name="ffn" + ("_mix" if mix is not None else "") + ("_final" if final else ""),
    )(*args)


def _shift_rows(y, carry_row):
    rolled = pltpu.roll(y, 1, 0)
    first = lax.broadcasted_iota(jnp.int32, y.shape, 0) == 0
    return jnp.where(first, carry_row, rolled)


def _inproj_kernel(*refs, has_vres, per_b, tm, col):
    refs = list(refs)
    x_ref, gain_ref, sh_ref, sc_ref, mu_ref, cw_ref, win_ref = refs[:7]
    del refs[:7]
    if has_vres:
        vmu_ref, wv_ref = refs[:2]
        del refs[:2]
    or_ref, oq_ref, oz_ref, ob_ref = refs[:4]
    del refs[:4]
    if has_vres:
        ov_ref = refs.pop(0)
    carry_r, ext = refs[:2]
    carry_v = refs[2] if has_vres else None
    pad = SUBLANE

    @pl.when(pl.program_id(0) % per_b == 0)
    def _():
        carry_r[...] = jnp.zeros_like(carry_r)
        ext[0:pad, :] = jnp.zeros((pad, GDN_QKV), F32)
        if has_vres:
            carry_v[...] = jnp.zeros_like(carry_v)

    h = _norm_mod(x_ref[...], gain_ref[...], sh_ref[0], sc_ref[0]).astype(BF16)
    blocks = lambda width: [(a, min(a + col, width)) for a in range(0, width, col)]
    for a, b in blocks(GDN_QKV):
        q = _dot(h, win_ref[:, RWKV_COLS + a:RWKV_COLS + b])
        ext[pad:pad + tm, a:b] = q
        conv = q * cw_ref[CONV_WIDTH - 1:CONV_WIDTH, a:b]
        for j in range(CONV_WIDTH - 1):
            back = CONV_WIDTH - 1 - j
            conv = conv + ext[pad - back:pad - back + tm, a:b] * cw_ref[j:j + 1, a:b]
        ext[0:pad, a:b] = q[tm - pad:tm, :]
        oq_ref[:, a:b] = _silu(conv)

    for a, b in blocks(RWKV_COLS):
        y = _dot(h, win_ref[:, a:b])
        prev = _shift_rows(y, carry_r[0:1, a:b])
        carry_r[0:1, a:b] = y[tm - 1:tm, :]
        or_ref[:, a:b] = y + (prev - y) * mu_ref[:, a:b]
    if has_vres:
        yv = _dot(h, wv_ref[...])
        prev_v = _shift_rows(yv, carry_v[0:1, :])
        carry_v[0:1, :] = yv[tm - 1:tm, :]
        ov_ref[...] = yv + (prev_v - yv) * vmu_ref[...]

    oz_ref[...] = _dot(h, win_ref[:, RWKV_COLS + GDN_QKV:RWKV_COLS + GDN_QKV + GDN_WIDTH])
    ba0 = RWKV_COLS + GDN_QKV + GDN_WIDTH
    ob_ref[...] = _dot(h, win_ref[:, ba0:ba0 + LANE])


def _inproj(x, gain, shift, scale, mu, conv_w, w_in, layer, seq, vres_mu=None, w_vres=None, tm=512,
            col=2 * MXU_TILE):
    t, d = x.shape
    tm = min(tm, seq)
    per_b = seq // tm
    has_vres = vres_mu is not None
    row = lambda i: (i, 0)
    once = dict(index_map=lambda i: (0, 0), pipeline_mode=pl.Buffered(1))
    modspec = pl.BlockSpec((1, 1, d), lambda i: (i // per_b, 0, 0))
    small = [gain.reshape(1, d), shift, scale, mu, conv_w]
    in_specs = [pl.BlockSpec((tm, d), row), pl.BlockSpec((1, d), **once), modspec, modspec,
                pl.BlockSpec(mu.shape, **once), pl.BlockSpec(conv_w.shape, **once)]
    in_specs += [pl.BlockSpec((None,) + w_in.shape[1:], lambda i: (layer, 0, 0), pipeline_mode=pl.Buffered(1))]
    args = [x] + small + [w_in]
    widths = [RWKV_COLS, GDN_QKV, GDN_WIDTH, LANE]
    scratch = [pltpu.VMEM((SUBLANE, RWKV_COLS), F32), pltpu.VMEM((tm + SUBLANE, GDN_QKV), F32)]
    if has_vres:
        in_specs += [pl.BlockSpec(vres_mu.shape, **once), pl.BlockSpec(w_vres.shape, **once)]
        args += [vres_mu, w_vres]
        widths.append(w_vres.shape[1])
        scratch.append(pltpu.VMEM((SUBLANE, LANE), F32))
    return pl.pallas_call(
        functools.partial(_inproj_kernel, has_vres=has_vres, per_b=per_b, tm=tm, col=col),
        grid=(t // tm,),
        in_specs=in_specs,
        out_specs=[pl.BlockSpec((tm, n), row) for n in widths],
        out_shape=[jax.ShapeDtypeStruct((t, n), F32) for n in widths],
        scratch_shapes=scratch,
        compiler_params=pltpu.CompilerParams(
            dimension_semantics=("arbitrary",), vmem_limit_bytes=VMEM_LIMIT),
        name="inproj",
    )(*args)


def _rwkv_body(refs, has_vres, tc):
    if has_vres:
        (y_ref, yv_ref, vf_ref, w0_ref, wup_ref, a0_ref, aup_ref, gup_ref, kk_ref, ka_ref,
         rk_ref, gnw_ref, gnb_ref, vup_ref, v0_ref,
         o_ref, state, o_scr) = refs
    else:
        (y_ref, w0_ref, wup_ref, a0_ref, aup_ref, gup_ref, kk_ref, ka_ref,
         rk_ref, gnw_ref, gnb_ref,
         o_ref, v_ref, state, o_scr) = refs
    n, w = RWKV_HEAD_DIM, RWKV_WIDTH

    @pl.when(pl.program_id(1) == 0)
    def _():
        state[...] = jnp.zeros_like(state)

    yield
    ym = y_ref[...]
    r = ym[:, 0:w]
    k = ym[:, w:2 * w]
    v = ym[:, 2 * w:3 * w]
    wd = ym[:, 3 * w:3 * w + DECAY_RANK]
    ad = ym[:, 3 * w + DECAY_RANK:3 * w + DECAY_RANK + ICLR_RANK]
    gd = ym[:, 3 * w + DECAY_RANK + ICLR_RANK:]

    if has_vres:
        logit = v0_ref[...] + _dot(yv_ref[...].astype(BF16), vup_ref[...].astype(BF16))
        v = v + (vf_ref[...] - v) * _sigmoid(logit)
    else:
        v_ref[...] = v

    z = w0_ref[...] + _dot(jnp.tanh(wd).astype(BF16), wup_ref[...].astype(BF16))
    logw = -math.exp(-0.5) * _sigmoid(z)
    a = _sigmoid(a0_ref[...] + _dot(ad.astype(BF16), aup_ref[...].astype(BF16)))
    g = _dot(_sigmoid(gd).astype(BF16), gup_ref[...].astype(BF16))

    gw = MXU_TILE
    hpg = gw // n
    groups = [slice(gi * gw, (gi + 1) * gw) for gi in range(w // gw)]
    rr = lax.broadcasted_iota(jnp.int32, (gw, gw), 0)
    cc = lax.broadcasted_iota(jnp.int32, (gw, gw), 1)
    same_blk = (rr // CHUNK) == (cc // CHUNK)
    head_ones = (rr // n) == (cc // n)
    tt = lax.broadcasted_iota(jnp.int32, (CHUNK, gw), 0)
    ss = lax.broadcasted_iota(jnp.int32, (CHUNK, gw), 1) % CHUNK
    ls_strict = tt > ss
    ls_incl = tt >= ss
    eye_ls = (tt == ss).astype(F32)
    l_incl = _tri(CHUNK, False)

    def head_sum(x):
        hi, lo = _split2(x)
        rows = x.shape[0]
        parts = [p[:, ln] for p in (hi, lo) for ln in groups]
        r = jnp.dot(jnp.concatenate(parts, axis=0), head_ones.astype(BF16), preferred_element_type=F32)
        ng = len(groups)
        return jnp.concatenate([r[gi * rows:(gi + 1) * rows] + r[(ng + gi) * rows:(ng + gi + 1) * rows]
                                for gi in range(ng)], axis=1)

    def bd(x):
        return _stack_bd(x.astype(BF16), same_blk, hpg)

    def ls_dot(a, b_bd):
        return jnp.dot(a.astype(BF16), b_bd, preferred_element_type=F32)

    yield
    kkr = k * kk_ref[...]
    kk = kkr * lax.rsqrt(head_sum(kkr * kkr) + EPS_L2)
    k2 = k * (1.0 + (a - 1.0) * ka_ref[...])
    b = kk * a
    av = -kk

    n_chunks = tc // CHUNK
    insts = [(c, gi) for c in range(n_chunks) for gi in range(len(groups))]
    d = {}
    for c in range(n_chunks):
        rows = slice(c * CHUNK, (c + 1) * CHUNK)
        lw = logw[rows]
        cum = _sel_dot(l_incl, lw)
        last = cum[CHUNK - 1:CHUNK, :]
        e_pos = jnp.exp(cum)
        e_neg = jnp.exp(-cum)
        e_to_end = jnp.exp(last - cum)
        gam_end = jnp.exp(last)
        at = av[rows] * jnp.exp(cum - lw)
        rt = r[rows] * e_pos
        bt = b[rows] * e_neg
        kt = k2[rows] * e_neg
        bg = b[rows] * e_to_end
        kg = k2[rows] * e_to_end
        vc = v[rows]
        for gi, ln in enumerate(groups):
            d[c, gi] = dict(at=at[:, ln], rt=rt[:, ln], bt=bt[:, ln], kt=kt[:, ln], bg=bg[:, ln],
                            kg=kg[:, ln], v=vc[:, ln], gam=gam_end[:, ln])
    yield
    for i in insts:
        e = d[i]
        x_ar = jnp.concatenate([e['at'], e['rt']], axis=0).astype(BF16)
        y_bk = jnp.concatenate([bd(e['bt']), bd(e['kt'])], axis=0)
        m1 = _dg(x_ar, y_bk, NT)
        e['n'] = jnp.where(ls_strict, m1[:CHUNK, :gw], 0.0)
        e['a_ak'] = jnp.where(ls_strict, m1[:CHUNK, gw:], 0.0)
        e['a_rb'] = jnp.where(ls_incl, m1[CHUNK:, :gw], 0.0)
        e['a_rk'] = jnp.where(ls_incl, m1[CHUNK:, gw:], 0.0)
        e['t'] = eye_ls + e['n']
        e['v_bd'] = bd(e['v'])
    yield
    for i in insts:
        d[i]['n'] = ls_dot(d[i]['n'], bd(d[i]['n']))
    yield
    steps = CHUNK.bit_length() - 2
    for k in range(steps):
        for i in insts:
            e = d[i]
            if k + 1 < steps:
                both = ls_dot(jnp.concatenate([e['t'], e['n']], axis=0), bd(e['n']))
                e['t'] = e['t'] + both[:CHUNK]
                e['n'] = both[CHUNK:]
            else:
                e['t'] = e['t'] + ls_dot(e['t'], bd(e['n']))
        yield
    for i in insts:
        e = d[i]
        e['w'] = ls_dot(e['t'], bd(e['at']))
        both = ls_dot(jnp.concatenate([e['a_ak'], e['a_rk']], axis=0), e['v_bd'])
        e['akv'] = both[:CHUNK]
        e['arkv'] = both[CHUNK:]
    yield
    for i in insts:
        e = d[i]
        e['u0'] = ls_dot(e['t'], bd(e['akv']))
        e['q'] = e['rt'] + ls_dot(e['a_rb'], bd(e['w']))
    yield
    for i in insts:
        e = d[i]
        e['o0'] = ls_dot(e['a_rb'], bd(e['u0'])) + e['arkv']
        e['m'] = jnp.where(same_blk, _bdg(e['w'], e['bg'], TN), 0.0)
        uv = jnp.concatenate([e['u0'], e['v']], axis=0)
        bk_end = jnp.concatenate([e['bg'], e['kg']], axis=0)
        e['n0'] = jnp.where(same_blk, _bdg(uv, bk_end, TN), 0.0)
    yield
    for c, gi in insts:
        e = d[c, gi]
        s0 = state[gi]
        sb = s0.astype(BF16)
        o_scr[c * CHUNK:(c + 1) * CHUNK, groups[gi]] = _dg(e['q'].astype(BF16), sb, NT) + e['o0']
        state[gi] = s0 * e['gam'] + jnp.dot(sb, e['m'].astype(BF16), preferred_element_type=F32) + e['n0']
    yield

    o = o_scr[...]
    inv_n = 1.0 / n
    mean = head_sum(o) * inv_n
    dev = o - mean
    var = head_sum(dev * dev) * inv_n
    on = dev * lax.rsqrt(var + EPS_GN) * gnw_ref[...] + gnb_ref[...]
    bonus = head_sum(r * k2 * rk_ref[...])
    o_ref[...] = (on + bonus * v) * g


def _gdn_body(refs, tc):
    (qkv_ref, z_ref, ba_ref, alog_r_ref, dt_r_ref, alog_c_ref, dt_c_ref, nw_ref,
     o_ref, state, o_scr) = refs
    dh, nh, w = GDN_HEAD_DIM, GDN_HEADS, GDN_WIDTH
    pad = SUBLANE

    @pl.when(pl.program_id(1) == 0)
    def _():
        state[...] = jnp.zeros_like(state)

    yield
    qkv = qkv_ref[...]

    ba = ba_ref[...]
    beta_c = _sigmoid(ba)
    g_c = -jnp.exp(alog_r_ref[...]) * _softplus(ba + dt_r_ref[...])
    er = lax.broadcasted_iota(jnp.int32, (pad, LANE), 0)
    ec = lax.broadcasted_iota(jnp.int32, (pad, LANE), 1)
    pick = (er == ec).astype(BF16)
    ba_hi, ba_mid, ba_lo = _split3(ba)
    ba_t = _dg(pick, ba_hi, NT) + (_dg(pick, ba_mid, NT) + _dg(pick, ba_lo, NT))
    g_r = -jnp.exp(alog_c_ref[...]) * _softplus(ba_t + dt_c_ref[...])

    yield
    gw = nh * CHUNK
    rr = lax.broadcasted_iota(jnp.int32, (gw, gw), 0)
    cc = lax.broadcasted_iota(jnp.int32, (gw, gw), 1)
    same_blk = (rr // CHUNK) == (cc // CHUNK)
    u_bd = jnp.logical_and(same_blk, rr <= cc)
    tt = lax.broadcasted_iota(jnp.int32, (CHUNK, gw), 0)
    ss = lax.broadcasted_iota(jnp.int32, (CHUNK, gw), 1) % CHUNK
    ls_strict = tt > ss
    ls_incl = tt >= ss
    eye_ls = (tt == ss).astype(F32)
    l_incl = _tri(CHUNK, False)
    src = lax.broadcasted_iota(jnp.int32, (LANE, gw), 0)
    spread_g = src == nh + lax.broadcasted_iota(jnp.int32, (LANE, gw), 1) // CHUNK
    spread_b = src == lax.broadcasted_iota(jnp.int32, (LANE, gw), 1) // CHUNK
    srcw = lax.broadcasted_iota(jnp.int32, (LANE, w), 0)
    spread_gw = srcw == nh + lax.broadcasted_iota(jnp.int32, (LANE, w), 1) // dh
    spread_bw = srcw == lax.broadcasted_iota(jnp.int32, (LANE, w), 1) // dh
    head_rows = (lax.broadcasted_iota(jnp.int32, (gw, w), 0) // CHUNK
                 == lax.broadcasted_iota(jnp.int32, (gw, w), 1) // dh)

    def bd(xb):
        return _stack_bd(xb, same_blk, nh)

    def bd_wide(xb):
        return _stack_bd(xb, head_rows, nh)

    def ls_dot(a, y):
        return jnp.dot(a.astype(BF16), bd(y.astype(BF16)), preferred_element_type=F32)

    def l2norm_heads(x):
        parts = []
        for h in range(nh):
            xh = x[:, h * dh:(h + 1) * dh]
            parts.append(xh * lax.rsqrt(jnp.sum(xh * xh, axis=-1, keepdims=True) + EPS_L2))
        return jnp.concatenate(parts, axis=1)

    n_chunks = tc // CHUNK
    heads = [slice(h * dh, (h + 1) * dh) for h in range(nh)]
    d = []
    for c in range(n_chunks):
        rows = slice(c * CHUNK, (c + 1) * CHUNK)
        g_rows = g_c[rows]
        b_rows = beta_c[rows]
        cum_c = _sel_dot(l_incl, g_rows)
        acum = _dot_sel(cum_c, spread_g)
        cumw = _dot_sel(cum_c, spread_gw)
        beta = _dot_sel(b_rows, spread_b)
        betaw = _dot_sel(b_rows, spread_bw)
        g_row = jnp.concatenate([g_r[nh + h:nh + h + 1, rows] for h in range(nh)], axis=1)
        crow = _dot_sel(jnp.broadcast_to(g_row, (SUBLANE, gw)), u_bd)[0:1, :]
        decay = jnp.where(ls_incl, jnp.exp(jnp.where(ls_incl, acum - crow, 0.0)), 0.0)
        qc = l2norm_heads(qkv[rows, 0:w]) * (dh ** -0.5)
        kc = l2norm_heads(qkv[rows, w:2 * w])
        vc = qkv[rows, 2 * w:3 * w]
        kq = jnp.concatenate([kc, qc], axis=0).astype(BF16)
        m = _dg(kq, bd_wide(kc.astype(BF16)), NT)
        n = -jnp.where(ls_strict, m[:CHUNK] * decay * beta, 0.0)
        d.append(dict(n1=n, n=n, t=eye_ls + n, p=m[CHUNK:] * decay, cumw=cumw, betaw=betaw, qc=qc, kc=kc, vc=vc))
    yield
    for e in d:
        e['n'] = ls_dot(e['n'], e['n'])
    yield
    steps = CHUNK.bit_length() - 2
    for k in range(steps):
        for e in d:
            if k + 1 < steps:
                both = ls_dot(jnp.concatenate([e['t'], e['n']], axis=0), e['n'])
                e['t'] = e['t'] + both[:CHUNK]
                e['n'] = both[CHUNK:]
            else:
                e['t'] = e['t'] + ls_dot(e['t'], e['n'])
        yield
    for e in d:
        e['r'] = eye_ls - e['t'] + _dot3_bd(e['n1'], e['t'], bd)
    yield
    for e in d:
        e['t'] = e['t'] + ls_dot(e['t'], e['r'])
    yield
    for e in d:
        e_c = jnp.exp(e['cumw'])
        clast = e['cumw'][CHUNK - 1:CHUNK, :]
        tb = e['t'].astype(BF16)
        e['u'] = _dot(tb, bd_wide((e['vc'] * e['betaw']).astype(BF16)))
        e['wm'] = _dot(tb, bd_wide((e['kc'] * (e['betaw'] * e_c)).astype(BF16)))
        e['q_dec'] = e['qc'] * e_c
        e['k_dec'] = e['kc'] * jnp.exp(clast - e['cumw'])
        e['g'] = jnp.exp(clast)
    yield
    for e in d:
        pb = e['p'].astype(BF16)
        e['qe'] = e['q_dec'] - _dot(pb, bd_wide(e['wm'].astype(BF16)))
        e['o0'] = _dot(pb, bd_wide(e['u'].astype(BF16)))
        e['kw'] = [_bdg(e['k_dec'][:, ln], e['wm'][:, ln], TN) for ln in heads]
        e['n0'] = [_bdg(e['k_dec'][:, ln], e['u'][:, ln], TN) for ln in heads]
    yield
    for c, e in enumerate(d):
        outs = []
        for h, ln in enumerate(heads):
            s0 = state[h]
            lhs = jnp.concatenate([e['qe'][:, ln], e['kw'][h]], axis=0).astype(BF16)
            both = _dot(lhs, s0.astype(BF16))
            outs.append(both[:CHUNK] + e['o0'][:, ln])
            state[h] = s0 * e['g'][:, ln] - both[CHUNK:] + e['n0'][h]
        o_scr[c * CHUNK:(c + 1) * CHUNK, :] = jnp.concatenate(outs, axis=1)
    yield

    zs = _silu(z_ref[...])
    for h in range(nh):
        ln = slice(h * dh, (h + 1) * dh)
        o = o_scr[:, ln]
        o = o * lax.rsqrt(jnp.mean(o * o, axis=-1, keepdims=True) + EPS_RMS) * nw_ref[...]
        o_ref[:, ln] = o * zs[:, ln]


def _mixer_kernel(*refs, n_in_r, n_in_g, n_out_r, has_vres, tc):
    refs = list(refs)
    in_r, refs = refs[:n_in_r], refs[n_in_r:]
    in_g, refs = refs[:n_in_g], refs[n_in_g:]
    out_r, refs = refs[:n_out_r], refs[n_out_r:]
    out_g, refs = refs[:1], refs[1:]
    scr_r, scr_g = refs[:2], refs[2:]
    bodies = [_rwkv_body(in_r + out_r + scr_r, has_vres, tc), _gdn_body(in_g + out_g + scr_g, tc)]
    while bodies:
        for body in list(bodies):
            if next(body, bodies) is bodies:
                bodies.remove(body)


def _mixer(y_r, rparams, y_qkv, y_z, y_ba, gparams, seq, y_vres=None, v_first=None, vres_params=None, tc=512):
    t = y_r.shape[0]
    tc = min(tc, seq)
    per_b = seq // tc
    nb = t // seq
    has_vres = y_vres is not None
    w = RWKV_WIDTH
    row = lambda b, s: (b * per_b + s, 0)
    once = dict(index_map=lambda b, s: (0, 0), pipeline_mode=pl.Buffered(1))
    tile = lambda width: pl.BlockSpec((tc, width), row)

    in_r = [y_r] + ([y_vres, v_first] if has_vres else [])
    spec_r = [tile(a.shape[1]) for a in in_r]
    plist = list(rparams) + (list(vres_params) if has_vres else [])
    in_r += plist
    spec_r += [pl.BlockSpec(p.shape, **once) for p in plist]
    in_g = [y_qkv, y_z, y_ba] + list(gparams)
    spec_g = [tile(GDN_QKV), tile(GDN_WIDTH), tile(LANE)] + [pl.BlockSpec(p.shape, **once) for p in gparams]

    n_out_r = 1 if has_vres else 2
    n_out = n_out_r + 1
    scratch = [pltpu.VMEM((RWKV_WIDTH // MXU_TILE, MXU_TILE, MXU_TILE), F32), pltpu.VMEM((tc, w), F32),
               pltpu.VMEM((GDN_HEADS, GDN_HEAD_DIM, GDN_HEAD_DIM), F32), pltpu.VMEM((tc, GDN_WIDTH), F32)]
    return pl.pallas_call(
        functools.partial(_mixer_kernel, n_in_r=len(in_r), n_in_g=len(in_g), n_out_r=n_out_r,
                          has_vres=has_vres, tc=tc),
        grid=(nb, per_b),
        in_specs=spec_r + spec_g,
        out_specs=[tile(w)] * n_out,
        out_shape=[jax.ShapeDtypeStruct((t, w), F32)] * n_out,
        scratch_shapes=scratch,
        compiler_params=pltpu.CompilerParams(
            dimension_semantics=("parallel", "arbitrary"), vmem_limit_bytes=VMEM_LIMIT),
        name="mixer_vres" if has_vres else "mixer",
    )(*in_r, *in_g)


def _pad_cols(m, width):
    return jnp.pad(m, ((0, 0), (0, width - m.shape[1])))


def kernel(x, c, norm_gain, ada_w, ada_b, ffn_w_gu, ffn_w_down, w_in, w_out, rwkv_mu, rwkv_w0, rwkv_w_up, rwkv_a0, rwkv_a_up, rwkv_g_up, rwkv_k_k, rwkv_k_a, rwkv_r_k, rwkv_gn_w, rwkv_gn_b, vres_w_down, vres_mu, vres_w_up, vres_v0, gdn_conv_w, gdn_a_log, gdn_dt_bias, gdn_norm_w, final_gain):
    bsz, seq, d = x.shape
    depth = norm_gain.shape[0]
    nh = GDN_HEADS
    mod = _adaln(c, ada_w, ada_b)
    w_gu = ffn_w_gu.astype(BF16)
    w_dn = ffn_w_down.astype(BF16)
    w_in_b = jnp.pad(w_in, ((0, 0), (0, 0), (0, -w_in.shape[2] % LANE))).astype(BF16)
    xf = x.reshape(bsz * seq, d)
    v_first = None
    row = lambda p: p.reshape(1, -1)
    for l in range(depth):
        def mods(sub):
            m = mod[l * 3 + sub]
            return tuple(m[:, i * d:(i + 1) * d].reshape(bsz, 1, d) for i in range(3))

        shift, scale, gate = mods(0)
        xf = _ffn(xf, norm_gain[l, 0], shift, scale, gate, w_gu, w_dn, (l, 0), seq)

        shift, scale, gate = mods(1)
        ys = _inproj(xf, norm_gain[l, 1], shift, scale, row(rwkv_mu[l]), gdn_conv_w[l], w_in_b, l, seq,
                     vres_mu=_pad_cols(row(vres_mu[l - 1]), LANE) if l > 0 else None,
                     w_vres=_pad_cols(vres_w_down[l - 1], LANE).astype(BF16) if l > 0 else None)
        y_r, y_qkv, y_z, y_ba = ys[:4]

        rparams = [row(rwkv_w0[l]), rwkv_w_up[l], row(rwkv_a0[l]), rwkv_a_up[l],
                   rwkv_g_up[l], row(rwkv_k_k[l]), row(rwkv_k_a[l]), row(rwkv_r_k[l]),
                   row(rwkv_gn_w[l]), row(rwkv_gn_b[l])]
        alog = gdn_a_log[l].astype(F32)
        dtb = gdn_dt_bias[l].astype(F32)
        place = lambda p: jnp.pad(p, (nh, LANE - 2 * nh))
        gparams = [row(place(alog)), row(place(dtb)),
                   place(alog)[:SUBLANE].reshape(SUBLANE, 1), place(dtb)[:SUBLANE].reshape(SUBLANE, 1),
                   row(gdn_norm_w[l])]
        if l == 0:
            o_r, v_first, o_g = _mixer(y_r, rparams, y_qkv, y_z, y_ba, gparams, seq)
        else:
            vparams = [jnp.pad(vres_w_up[l - 1], ((0, LANE - VRES_RANK), (0, 0))),
                       row(vres_v0[l - 1])]
            o_r, o_g = _mixer(y_r, rparams, y_qkv, y_z, y_ba, gparams, seq,
                              y_vres=ys[4], v_first=v_first, vres_params=vparams)

        wo = w_out[l].astype(BF16)
        mix = (gate, o_r, o_g, wo[:RWKV_WIDTH], wo[RWKV_WIDTH:])
        shift, scale, gate = mods(2)
        xf = _ffn(xf, norm_gain[l, 2], shift, scale, gate, w_gu, w_dn, (l, 1), seq,
                  final_gain=final_gain if l == depth - 1 else None, mix=mix)
    return xf.reshape(bsz, seq, d)
```
